```python
import jax
import jax.numpy as jnp
from jax import lax
import numpy as np

D_MODEL = 1024
BATCH = 4
SEQ = 8192
DEPTH = 4
DEC_BATCH = 32
DEC_SEQ = 32
PAST_LEN = 2048

CHUNK = 64
N_EVEN = (DEPTH + 1) // 2
N_ODD = DEPTH // 2
EPS = 1e-6

D_A = D_MODEL // 2
CONV_W = 31
CONV_PAD = CONV_W - 1

H_B = 4
DK_B = D_MODEL // 16
DV_B = D_MODEL // 8
D_BK = H_B * DK_B
D_BV = H_B * DV_B
GATE_RANK = 16
GATE_TAU = 16.0
GLA_CHUNK = CHUNK

D_C = D_MODEL
POOL_WINDOWS = (2, 4, 8, 16)
N_POOL_GROUPS = 4
POOL_GW = D_C // N_POOL_GROUPS
POOL_PAD = 15

EV_SPLITS = (D_A, 2 * D_A, 3 * D_A, 3 * D_A + D_BK, 3 * D_A + 2 * D_BK,
             3 * D_A + 2 * D_BK + D_BV, 3 * D_A + 2 * D_BK + 2 * D_BV)
EV_IN = 3 * D_A + 2 * D_BK + 2 * D_BV + GATE_RANK

kernel_name = "hybrid_conv_gla_pool_streaming_step"


def rms_norm(x, g):
    xf = x.astype(jnp.float32)
    y = xf * lax.rsqrt(jnp.mean(xf * xf, axis=-1, keepdims=True) + EPS)
    return (y * g.astype(jnp.float32)).astype(x.dtype)


def layer_norm(x, g, b):
    xf = x.astype(jnp.float32)
    xc = xf - jnp.mean(xf, axis=-1, keepdims=True)
    y = xc * lax.rsqrt(jnp.mean(xc * xc, axis=-1, keepdims=True) + EPS)
    return (y * g.astype(jnp.float32) + b.astype(jnp.float32)).astype(x.dtype)


def causal_depthwise_conv(glu, buf, w, b):
    xp = jnp.concatenate([buf.astype(glu.dtype), glu], axis=1)
    y = lax.conv_general_dilated(
        xp, w.astype(glu.dtype)[:, None, :], window_strides=(1,), padding="VALID",
        dimension_numbers=("NWC", "WIO", "NWC"), feature_group_count=D_A)
    return y + b.astype(glu.dtype), xp[:, -CONV_PAD:]


def gla_blocked(q, k, v, log_a, s0):
    B, L = q.shape[0], q.shape[1]
    C = min(GLA_CHUNK, L)
    N = L // C
    f32 = jnp.float32
    qc = q.astype(f32).reshape(B, N, C, H_B, DK_B) * (DK_B ** -0.5)
    kc = k.astype(f32).reshape(B, N, C, H_B, DK_B)
    vc = v.astype(f32).reshape(B, N, C, H_B, DV_B)
    b = jnp.cumsum(log_a.astype(f32).reshape(B, N, C, H_B, DK_B), axis=2)
    b_last = b[:, :, -1:]
    q_e = qc * jnp.exp(b)
    k_e = kc * jnp.exp(-b)
    k_l = kc * jnp.exp(b_last - b)
    mask = jnp.tril(jnp.ones((C, C), dtype=bool))
    att = jnp.where(mask, jnp.einsum("bnthd,bnshd->bnhts", q_e, k_e), 0.0)
    o = jnp.einsum("bnhts,bnshv->bnthv", att, vc)
    upd = jnp.einsum("bnshd,bnshv->bnhdv", k_l, vc)
    decay = jnp.exp(b_last[:, :, 0])

    def step(s, inp):
        d, u = inp
        return d[..., None] * s + u, s

    s_fin, s_start = lax.scan(step, s0.astype(f32),
                              (jnp.moveaxis(decay, 1, 0), jnp.moveaxis(upd, 1, 0)))
    s_start = jnp.moveaxis(s_start, 0, 1)
    o = o + jnp.einsum("bnthd,bnhdv->bnthv", q_e, s_start)
    return o.reshape(B, L, H_B, DV_B), s_fin.astype(s0.dtype)


def even_mixer(h, conv_buf, gla_s, w_in, conv_w, conv_b, ln_g, ln_b, gate_w2, gate_b, head_g, w_out):
    B, L, _ = h.shape
    a_val, a_glu, a_gate, q, k, v, b_gate, lr = jnp.split(h @ w_in, EV_SPLITS, axis=-1)
    glu = a_val * jax.nn.sigmoid(a_glu)
    ya, new_buf = causal_depthwise_conv(glu, conv_buf, conv_w, conv_b)
    ya = jax.nn.silu(layer_norm(ya, ln_g, ln_b)) * jax.nn.silu(a_gate)
    log_a = jax.nn.log_sigmoid((lr @ gate_w2 + gate_b).astype(jnp.float32)) / GATE_TAU
    o, new_s = gla_blocked(q.reshape(B, L, H_B, DK_B), k.reshape(B, L, H_B, DK_B),
                           v.reshape(B, L, H_B, DV_B), log_a.reshape(B, L, H_B, DK_B), gla_s)
    o = rms_norm(o, head_g.reshape(H_B, DV_B)).reshape(B, L, D_BV).astype(h.dtype)
    yb = o * jax.nn.silu(b_gate)
    y = jnp.concatenate([ya, yb], axis=-1) @ w_out
    return y, new_buf, new_s


def pool_mixer(h, pool_buf, pos0, w_in, group_w, group_b, scale, w_out):
    B, L, _ = h.shape
    v, gate = jnp.split(h @ w_in, 2, axis=-1)
    xp = jnp.concatenate([pool_buf.astype(v.dtype), v], axis=1)
    cs = jnp.cumsum(xp.astype(jnp.float32), axis=1)
    cs = jnp.pad(cs, ((0, 0), (1, 0), (0, 0)))
    pos = pos0 + jnp.arange(L)
    vf = v.astype(jnp.float32)
    outs = []
    for gi, w in enumerate(POOL_WINDOWS):
        sl = slice(gi * POOL_GW, (gi + 1) * POOL_GW)
        win_sum = cs[:, POOL_PAD + 1:POOL_PAD + 1 + L, sl] - cs[:, POOL_PAD + 1 - w:POOL_PAD + 1 - w + L, sl]
        cnt = jnp.minimum(pos + 1, w).astype(jnp.float32)
        outs.append(win_sum / cnt[None, :, None] - vf[..., sl])
    p = jnp.concatenate(outs, axis=-1).astype(h.dtype).reshape(B, L, N_POOL_GROUPS, POOL_GW)
    p = jnp.einsum("blgc,gcd->blgd", p, group_w).reshape(B, L, D_C) + group_b
    y = (p * scale * jax.nn.silu(gate)) @ w_out
    return y, xp[:, -POOL_PAD:]


def trunk(x, c, conv_st, gla_st, pool_st, pos0,
          norm_g, ada_w, ada_b, ev_w_in, ev_conv_w, ev_conv_b, ev_ln_g, ev_ln_b,
          ev_gate_w2, ev_gate_b, ev_head_g, ev_w_out,
          od_w_in, od_group_w, od_group_b, od_scale, od_w_out, final_g):
    cs = jax.nn.silu(c)
    conv_out, gla_out, pool_out = [], [], []
    for l in range(DEPTH):
        shift, scale, gate = jnp.split(cs @ ada_w[l] + ada_b[l], 3, axis=-1)
        h = rms_norm(x, norm_g[l]) * (1.0 + scale[:, None, :]) + shift[:, None, :]
        if l % 2 == 0:
            e = l // 2
            y, cb, gs = even_mixer(h, conv_st[e], gla_st[e], ev_w_in[e], ev_conv_w[e], ev_conv_b[e],
                                   ev_ln_g[e], ev_ln_b[e], ev_gate_w2[e], ev_gate_b[e],
                                   ev_head_g[e], ev_w_out[e])
            conv_out.append(cb)
            gla_out.append(gs)
        else:
            o = l // 2
            y, pb = pool_mixer(h, pool_st[o], pos0, od_w_in[o], od_group_w[o], od_group_b[o],
                               od_scale[o], od_w_out[o])
            pool_out.append(pb)
        x = x + gate[:, None, :] * y
    return rms_norm(x, final_g), jnp.stack(conv_out), jnp.stack(gla_out), jnp.stack(pool_out)


def setup_inputs(seed: int = 0) -> dict:
    key = jax.random.key(seed)
    ks = jax.random.split(key, 25)

    def nrm(k, shape, s):
        return jax.random.normal(k, shape, jnp.float32) * s

    return {
        "x_prompt": nrm(ks[0], (BATCH, SEQ, D_MODEL), 1.0),
        "x_sample": nrm(ks[1], (DEC_BATCH, DEC_SEQ, D_MODEL), 1.0),
        "c_prompt": nrm(ks[2], (BATCH, D_MODEL), 1.0),
        "c_sample": nrm(ks[3], (DEC_BATCH, D_MODEL), 1.0),
        "state_conv": nrm(ks[4], (N_EVEN, DEC_BATCH, CONV_PAD, D_A), 0.5),
        "state_gla": nrm(ks[5], (N_EVEN, DEC_BATCH, H_B, DK_B, DV_B), 1.0),
        "state_pool": nrm(ks[6], (N_ODD, DEC_BATCH, POOL_PAD, D_C), 1.0),
        "norm_g": 1.0 + nrm(ks[7], (DEPTH, D_MODEL), 0.05),
        "ada_w": nrm(ks[8], (DEPTH, D_MODEL, 3 * D_MODEL), 0.5 * D_MODEL ** -0.5),
        "ada_b": nrm(ks[9], (DEPTH, 3 * D_MODEL), 0.02),
        "ev_w_in": nrm(ks[10], (N_EVEN, D_MODEL, EV_IN), D_MODEL ** -0.5),
        "ev_conv_w": nrm(ks[11], (N_EVEN, CONV_W, D_A), CONV_W ** -0.5),
        "ev_conv_b": nrm(ks[12], (N_EVEN, D_A), 0.02),
        "ev_ln_g": 1.0 + nrm(ks[13], (N_EVEN, D_A), 0.05),
        "ev_ln_b": nrm(ks[14], (N_EVEN, D_A), 0.02),
        "ev_gate_w2": nrm(ks[15], (N_EVEN, GATE_RANK, D_BK), GATE_RANK ** -0.5),
        "ev_gate_b": nrm(ks[16], (N_EVEN, D_BK), 0.1),
        "ev_head_g": 1.0 + nrm(ks[17], (N_EVEN, D_BV), 0.05),
        "ev_w_out": nrm(ks[18], (N_EVEN, D_A + D_BV, D_MODEL), (D_A + D_BV) ** -0.5),
        "od_w_in": nrm(ks[19], (N_ODD, D_MODEL, 2 * D_C), D_MODEL ** -0.5),
        "od_group_w": nrm(ks[20], (N_ODD, N_POOL_GROUPS, POOL_GW, POOL_GW), POOL_GW ** -0.5),
        "od_group_b": nrm(ks[21], (N_ODD, D_C), 0.02),
        "od_scale": 1.0 + nrm(ks[22], (N_ODD, D_C), 0.1),
        "od_w_out": nrm(ks[23], (N_ODD, D_C, D_MODEL), D_C ** -0.5),
        "final_g": 1.0 + nrm(ks[24], (D_MODEL,), 0.05),
    }


def reference(x_prompt, x_sample, c_prompt, c_sample, state_conv, state_gla, state_pool,
              norm_g, ada_w, ada_b, ev_w_in, ev_conv_w, ev_conv_b, ev_ln_g, ev_ln_b,
              ev_gate_w2, ev_gate_b, ev_head_g, ev_w_out,
              od_w_in, od_group_w, od_group_b, od_scale, od_w_out, final_g):
    b_p = x_prompt.shape[0]
    zero_conv = jnp.zeros((N_EVEN, b_p, CONV_PAD, D_A), x_prompt.dtype)
    zero_gla = jnp.zeros((N_EVEN, b_p, H_B, DK_B, DV_B), state_gla.dtype)
    zero_pool = jnp.zeros((N_ODD, b_p, POOL_PAD, D_C), x_prompt.dtype)
    y_prompt, conv_p, gla_p, pool_p = trunk(
        x_prompt, c_prompt, zero_conv, zero_gla, zero_pool, 0,
        norm_g, ada_w, ada_b, ev_w_in, ev_conv_w, ev_conv_b, ev_ln_g, ev_ln_b,
        ev_gate_w2, ev_gate_b, ev_head_g, ev_w_out,
        od_w_in, od_group_w, od_group_b, od_scale, od_w_out, final_g)
    y_sample, conv_s, gla_s, pool_s = trunk(
        x_sample, c_sample, state_conv, state_gla, state_pool, PAST_LEN,
        norm_g, ada_w, ada_b, ev_w_in, ev_conv_w, ev_conv_b, ev_ln_g, ev_ln_b,
        ev_gate_w2, ev_gate_b, ev_head_g, ev_w_out,
        od_w_in, od_group_w, od_group_b, od_scale, od_w_out, final_g)
    return (y_prompt, y_sample, conv_p, gla_p, pool_p, conv_s, gla_s, pool_s)
```

```python
import functools

import numpy as np
import jax
import jax.numpy as jnp
from jax import lax
from jax.experimental import pallas as pl
from jax.experimental.pallas import tpu as pltpu

F32 = jnp.float32
BF16 = jnp.bfloat16

D_MODEL = 1024
DEPTH = 4
EPS = 1e-6
PAST_LEN = 2048

D_A = D_MODEL // 2
CONV_W = 31
CONV_PAD = CONV_W - 1
CONV_HIST = 32

H_B = 4
DK_B = D_MODEL // 16
DV_B = D_MODEL // 8
D_BK = H_B * DK_B
D_BV = H_B * DV_B
GATE_RANK = 16
GATE_TAU = 16.0
GLA_CHUNK = 64
LANES = 128
GROUP_ROWS = 256

D_C = D_MODEL
POOL_WINDOWS = (2, 4, 8, 16)
POOL_GW = D_C // 4
POOL_PAD = 15
POOL_HIST = 16

O_VAL, O_GLU, O_GATE = 0, D_A, 2 * D_A
O_Q = 3 * D_A
O_K = O_Q + D_BK
O_V = O_K + D_BK
O_BG = O_V + D_BV
O_LR = O_BG + D_BV
EV_IN = O_LR + GATE_RANK
EV_IN_PAD = O_LR + LANES

VMEM_LIMIT = 56 * 1024 * 1024

CONV_RB = 16
POOL_RB = 32


def _sigmoid(x):
    return 1.0 / (1.0 + jnp.exp(-x))


def _silu(x):
    return x * _sigmoid(x)


def _split_bf16(x):
    hi = x.astype(BF16)
    lo = (x - hi.astype(F32)).astype(BF16)
    return hi, lo


def _dot(a, b):
    return jnp.dot(a, b, preferred_element_type=F32)


def _dot_nt(a, b):
    return lax.dot_general(a, b, (((1,), (1,)), ((), ())), preferred_element_type=F32)


def _dot_tn(a, b):
    return lax.dot_general(a, b, (((0,), (0,)), ((), ())), preferred_element_type=F32)


def _const_spec(shape):
    nd = len(shape)
    return pl.BlockSpec(shape, lambda *_: (0,) * nd, pipeline_mode=pl.Buffered(1))


def _ada_kernel(c_ref, w_ref, b_ref, o_ref):
    c = c_ref[...]
    cs = _silu(c)
    cs_hi, cs_lo = _split_bf16(cs)
    w_hi, w_lo = _split_bf16(w_ref[0])
    acc = _dot(cs_hi, w_hi) + (_dot(cs_lo, w_hi) + _dot(cs_hi, w_lo))
    o_ref[0, 0] = acc + b_ref[0, 0]


def _ada_call(c_all, ada_w, ada_b):
    bp = c_all.shape[0]
    return pl.pallas_call(
        _ada_kernel,
        grid=(DEPTH, 3),
        in_specs=[
            pl.BlockSpec((bp, D_MODEL), lambda l, j: (0, 0)),
            pl.BlockSpec((1, D_MODEL, D_MODEL), lambda l, j: (l, 0, j)),
            pl.BlockSpec((1, 1, 1, D_MODEL), lambda l, j: (l, j, 0, 0)),
        ],
        out_specs=pl.BlockSpec((1, 1, bp, D_MODEL), lambda l, j: (l, j, 0, 0)),
        out_shape=jax.ShapeDtypeStruct((DEPTH, 3, bp, D_MODEL), F32),
        compiler_params=pltpu.CompilerParams(
            dimension_semantics=("arbitrary", "arbitrary"), vmem_limit_bytes=VMEM_LIMIT),
        name="ada_mod",
    )(c_all, ada_w, ada_b.reshape(DEPTH, 3, 1, D_MODEL))


def _modulated_input(x_ref, mod_ref, ng_ref):
    x3 = x_ref[...]
    nb, tl, d = x3.shape
    shift = mod_ref[0]
    scale = mod_ref[1]
    ms = jnp.mean(x3 * x3, axis=-1, keepdims=True)
    gs = ng_ref[...] * (1.0 + scale)
    h = (x3 * lax.rsqrt(ms + EPS)) * gs + shift
    return h.astype(BF16).reshape(nb * tl, d)


def _even_kernel(x_ref, mod_ref, cst_ref, gst_ref, ng_ref, win_ref, cw_ref, cb_ref, lng_ref,
                 lnb_ref, gw2_ref, gb_ref, hg_ref, wout_ref, tri_ref, ones_ref,
                 xo_ref, co_ref, go_ref,
                 proj_scr, xp_scr, st_scr, o_scr, y_scr, *, nb, tl, chunk):
    t = pl.program_id(1)
    last = pl.num_programs(1) - 1
    m = nb * tl
    gr = min(m, GROUP_ROWS)

    lane = lax.broadcasted_iota(jnp.int32, (1, LANES), 1)
    lo_half = lane < DK_B

    @pl.when(t == 0)
    def _init():
        xp_scr[:, CONV_HIST - CONV_PAD:CONV_HIST, :] = cst_ref[...]
        for n in range(nb):
            for hh in range(H_B):
                s = gst_ref[n, hh]
                z = jnp.zeros_like(s)
                wide = jnp.concatenate([s, z] if hh % 2 == 0 else [z, s], axis=0)
                st_scr[n, hh] = wide.T

    hb = _modulated_input(x_ref, mod_ref, ng_ref)
    proj_scr[...] = _dot(hb, win_ref[...])

    glu = proj_scr[:, O_VAL:O_VAL + D_A] * _sigmoid(proj_scr[:, O_GLU:O_GLU + D_A])
    xp_scr[:, CONV_HIST:CONV_HIST + tl, :] = glu.reshape(nb, tl, D_A)
    cb = cb_ref[...]
    lng = lng_ref[...]
    lnb = lnb_ref[...]
    base = CONV_HIST - CONV_PAD
    for n in range(nb):
        for rb in range(tl // CONV_RB):
            r0 = rb * CONV_RB
            acc = jnp.broadcast_to(cb, (CONV_RB, D_A))
            for j in range(CONV_W):
                acc = acc + cw_ref[j:j + 1, :] * xp_scr[n, base + j + r0:base + j + r0 + CONV_RB, :]
            mu = jnp.mean(acc, axis=-1, keepdims=True)
            xc = acc - mu
            var = jnp.mean(xc * xc, axis=-1, keepdims=True)
            yn = xc * lax.rsqrt(var + EPS) * lng + lnb
            rows = slice(n * tl + r0, n * tl + r0 + CONV_RB)
            ya = _silu(yn) * _silu(proj_scr[rows, O_GATE:O_GATE + D_A])
            y_scr[rows, 0:D_A] = ya.astype(BF16)
    xp_scr[:, 0:CONV_HIST, :] = xp_scr[:, tl:tl + CONV_HIST, :]

    pre = _dot(proj_scr[:, O_LR:O_LR + LANES].astype(BF16), gw2_ref[...]) + gb_ref[...]
    la = (jnp.minimum(pre, 0.0) - jnp.log1p(jnp.exp(-jnp.abs(pre)))) * (1.0 / GATE_TAU)
    la_hi, la_lo = _split_bf16(la)
    tri = tri_ref[...]
    ones_bd = ones_ref[...]
    ri = lax.broadcasted_iota(jnp.int32, (chunk, chunk), 0)
    ci = lax.broadcasted_iota(jnp.int32, (chunk, chunk), 1)
    causal = ri >= ci
    pair_lo = jnp.concatenate([lo_half, lo_half], axis=1)
    for g in range(m // gr):
        rows = slice(g * gr, (g + 1) * gr)
        b = _dot(tri, la_hi[rows]) + _dot(tri, la_lo[rows])
        bl = _dot(ones_bd, la_hi[rows]) + _dot(ones_bd, la_lo[rows])
        k = proj_scr[rows, O_K:O_K + D_BK]
        q_e = proj_scr[rows, O_Q:O_Q + D_BK] * (DK_B ** -0.5) * jnp.exp(b)
        q_even = jnp.where(pair_lo, q_e, 0.0).astype(BF16)
        q_odd = jnp.where(pair_lo, 0.0, q_e).astype(BF16)
        k_e = (k * jnp.exp(-b)).astype(BF16)
        k_l = (k * jnp.exp(bl - b)).astype(BF16)
        dec = jnp.exp(bl)
        vb = proj_scr[rows, O_V:O_V + D_BV].astype(BF16)
        for c in range(gr // chunk):
            cr = slice(c * chunk, (c + 1) * chunk)
            row0 = g * gr + c * chunk
            n = row0 // tl
            for hh in range(H_B):
                pc = slice((hh // 2) * LANES, (hh // 2 + 1) * LANES)
                vc = slice(hh * DV_B, (hh + 1) * DV_B)
                qh = (q_even if hh % 2 == 0 else q_odd)[cr, pc]
                att = _dot_nt(qh, k_e[cr, pc])
                att = jnp.where(causal, att, 0.0).astype(BF16)
                s_t = st_scr[n, hh]
                o = _dot(att, vb[cr, vc]) + _dot_nt(qh, s_t.astype(BF16))
                o_scr[row0:row0 + chunk, vc] = o
                upd = _dot_tn(vb[cr, vc], k_l[cr, pc])
                keep = lo_half if hh % 2 == 0 else jnp.logical_not(lo_half)
                st_scr[n, hh] = dec[c * chunk:c * chunk + 1, pc] * s_t + jnp.where(keep, upd, 0.0)

    for hh in range(H_B):
        vc = slice(hh * DV_B, (hh + 1) * DV_B)
        oh = o_scr[:, vc]
        ms = jnp.mean(oh * oh, axis=-1, keepdims=True)
        on = oh * lax.rsqrt(ms + EPS) * hg_ref[:, vc]
        yb = on * _silu(proj_scr[:, O_BG + hh * DV_B:O_BG + (hh + 1) * DV_B])
        y_scr[:, D_A + hh * DV_B:D_A + (hh + 1) * DV_B] = yb.astype(BF16)

    y = _dot(y_scr[...], wout_ref[...])
    xo_ref[...] = x_ref[...] + mod_ref[2] * y.reshape(nb, tl, D_MODEL)

    @pl.when(t == last)
    def _fin():
        co_ref[...] = xp_scr[:, CONV_HIST - CONV_PAD:CONV_HIST, :]
        for n in range(nb):
            for hh in range(H_B):
                s = st_scr[n, hh].T
                off = (hh % 2) * DK_B
                go_ref[n, hh] = s[off:off + DK_B, :]


def _decay_matrices(rows, chunk):
    i = np.arange(rows)
    same = (i[:, None] // chunk) == (i[None, :] // chunk)
    tri = same & (i[:, None] >= i[None, :])
    return jnp.asarray(tri, BF16), jnp.asarray(same, BF16)


def _even_call(x, mod, conv_st, gla_st, ng, w_in, cw, cb, lng, lnb, gw2, gb, hg, w_out, *, nb, tl):
    b, l, d = x.shape
    chunk = min(GLA_CHUNK, l)
    m = nb * tl
    gr = min(m, GROUP_ROWS)
    assert b % nb == 0 and l % tl == 0 and tl % chunk == 0 and m % gr == 0 and gr % chunk == 0
    assert tl >= CONV_HIST and tl % CONV_RB == 0
    tri, ones_bd = _decay_matrices(gr, chunk)
    kern = functools.partial(_even_kernel, nb=nb, tl=tl, chunk=chunk)
    return pl.pallas_call(
        kern,
        grid=(b // nb, l // tl),
        in_specs=[
            pl.BlockSpec((nb, tl, d), lambda i, t: (i, t, 0)),
            pl.BlockSpec((3, nb, 1, d), lambda i, t: (0, i, 0, 0)),
            pl.BlockSpec((nb, CONV_PAD, D_A), lambda i, t: (i, 0, 0)),
            pl.BlockSpec((nb, H_B, DK_B, DV_B), lambda i, t: (i, 0, 0, 0)),
            _const_spec((1, d)),
            _const_spec((d, EV_IN_PAD)),
            _const_spec((CONV_W, D_A)),
            _const_spec((1, D_A)),
            _const_spec((1, D_A)),
            _const_spec((1, D_A)),
            _const_spec((LANES, D_BK)),
            _const_spec((1, D_BK)),
            _const_spec((1, D_BV)),
            _const_spec((D_A + D_BV, d)),
            _const_spec((gr, gr)),
            _const_spec((gr, gr)),
        ],
        out_specs=[
            pl.BlockSpec((nb, tl, d), lambda i, t: (i, t, 0)),
            pl.BlockSpec((nb, CONV_PAD, D_A), lambda i, t: (i, 0, 0)),
            pl.BlockSpec((nb, H_B, DK_B, DV_B), lambda i, t: (i, 0, 0, 0)),
        ],
        out_shape=[
            jax.ShapeDtypeStruct((b, l, d), F32),
            jax.ShapeDtypeStruct((b, CONV_PAD, D_A), F32),
            jax.ShapeDtypeStruct((b, H_B, DK_B, DV_B), F32),
        ],
        scratch_shapes=[
            pltpu.VMEM((m, EV_IN_PAD), F32),
            pltpu.VMEM((nb, CONV_HIST + tl, D_A), F32),
            pltpu.VMEM((nb, H_B, DV_B, LANES), F32),
            pltpu.VMEM((m, D_BV), F32),
            pltpu.VMEM((m, D_A + D_BV), BF16),
        ],
        compiler_params=pltpu.CompilerParams(
            dimension_semantics=("arbitrary", "arbitrary"), vmem_limit_bytes=VMEM_LIMIT),
        name="even_layer",
    )(x, mod, conv_st, gla_st, ng, w_in, cw, cb, lng, lnb, gw2, gb, hg, w_out, tri, ones_bd)


def _odd_kernel(x_ref, mod_ref, pst_ref, ng_ref, win_ref, gw_ref, gb_ref, sc_ref, wout_ref, fg_ref,
                xo_ref, po_ref,
                proj_scr, vp_scr, p_scr, z_scr, *, nb, tl, pos0, final):
    t = pl.program_id(1)
    last = pl.num_programs(1) - 1

    @pl.when(t == 0)
    def _init():
        vp_scr[:, POOL_HIST - POOL_PAD:POOL_HIST, :] = pst_ref[...]

    hb = _modulated_input(x_ref, mod_ref, ng_ref)
    proj_scr[...] = _dot(hb, win_ref[...])
    vp_scr[:, POOL_HIST:POOL_HIST + tl, :] = proj_scr[:, 0:D_C].reshape(nb, tl, D_C)

    row = lax.broadcasted_iota(jnp.int32, (POOL_RB, POOL_GW), 0)
    for n in range(nb):
        for rb in range(tl // POOL_RB):
            r0 = rb * POOL_RB
            pos1 = row + (pos0 + 1 + r0) + t * tl
            for gi, w in enumerate(POOL_WINDOWS):
                cs = slice(gi * POOL_GW, (gi + 1) * POOL_GW)
                cur = vp_scr[n, POOL_HIST + r0:POOL_HIST + r0 + POOL_RB, cs]
                acc = cur
                for i in range(1, w):
                    acc = acc + vp_scr[n, POOL_HIST + r0 - i:POOL_HIST + r0 - i + POOL_RB, cs]
                cnt = jnp.minimum(pos1, w).astype(F32)
                p = acc / cnt - cur
                p_scr[n * tl + r0:n * tl + r0 + POOL_RB, cs] = p.astype(BF16)
    vp_scr[:, 0:POOL_HIST, :] = vp_scr[:, tl:tl + POOL_HIST, :]

    for gi in range(len(POOL_WINDOWS)):
        cs = slice(gi * POOL_GW, (gi + 1) * POOL_GW)
        pg = _dot(p_scr[:, cs], gw_ref[gi]) + gb_ref[:, cs]
        z = pg * sc_ref[:, cs] * _silu(proj_scr[:, D_C + gi * POOL_GW:D_C + (gi + 1) * POOL_GW])
        z_scr[:, cs] = z.astype(BF16)

    y = _dot(z_scr[...], wout_ref[...])
    xo = x_ref[...] + mod_ref[2] * y.reshape(nb, tl, D_MODEL)
    if final:
        ms = jnp.mean(xo * xo, axis=-1, keepdims=True)
        xo = xo * lax.rsqrt(ms + EPS) * fg_ref[...]
    xo_ref[...] = xo

    @pl.when(t == last)
    def _fin():
        po_ref[...] = vp_scr[:, POOL_HIST - POOL_PAD:POOL_HIST, :]


def _odd_call(x, mod, pool_st, ng, w_in, gw, gb, sc, w_out, fg, *, nb, tl, pos0, final):
    b, l, d = x.shape
    m = nb * tl
    assert b % nb == 0 and l % tl == 0 and tl >= POOL_HIST and tl % POOL_RB == 0
    kern = functools.partial(_odd_kernel, nb=nb, tl=tl, pos0=pos0, final=final)
    return pl.pallas_call(
        kern,
        grid=(b // nb, l // tl),
        in_specs=[
            pl.BlockSpec((nb, tl, d), lambda i, t: (i, t, 0)),
            pl.BlockSpec((3, nb, 1, d), lambda i, t: (0, i, 0, 0)),
            pl.BlockSpec((nb, POOL_PAD, D_C), lambda i, t: (i, 0, 0)),
            _const_spec((1, d)),
            _const_spec((d, 2 * D_C)),
            _const_spec((len(POOL_WINDOWS), POOL_GW, POOL_GW)),
            _const_spec((1, D_C)),
            _const_spec((1, D_C)),
            _const_spec((D_C, d)),
            _const_spec((1, d)),
        ],
        out_specs=[
            pl.BlockSpec((nb, tl, d), lambda i, t: (i, t, 0)),
            pl.BlockSpec((nb, POOL_PAD, D_C), lambda i, t: (i, 0, 0)),
        ],
        out_shape=[
            jax.ShapeDtypeStruct((b, l, d), F32),
            jax.ShapeDtypeStruct((b, POOL_PAD, D_C), F32),
        ],
        scratch_shapes=[
            pltpu.VMEM((m, 2 * D_C), F32),
            pltpu.VMEM((nb, POOL_HIST + tl, D_C), F32),
            pltpu.VMEM((m, D_C), BF16),
            pltpu.VMEM((m, D_C), BF16),
        ],
        compiler_params=pltpu.CompilerParams(
            dimension_semantics=("arbitrary", "arbitrary"), vmem_limit_bytes=VMEM_LIMIT),
        name="odd_layer",
    )(x, mod, pool_st, ng, w_in, gw, gb, sc, w_out, fg)


PROMPT_TILE = dict(nb=1, tl=256)
SAMPLE_TILE = dict(nb=8, tl=32)


def kernel(x_prompt, x_sample, c_prompt, c_sample, state_conv, state_gla, state_pool, norm_g, ada_w, ada_b, ev_w_in, ev_conv_w, ev_conv_b, ev_ln_g, ev_ln_b, ev_gate_w2, ev_gate_b, ev_head_g, ev_w_out, od_w_in, od_group_w, od_group_b, od_scale, od_w_out, final_g):
    b_p = x_prompt.shape[0]
    b_s = x_sample.shape[0]
    n_even = ev_w_in.shape[0]
    n_odd = od_w_in.shape[0]

    c_all = jnp.concatenate([c_prompt, c_sample], axis=0)
    bp = -(-c_all.shape[0] // 8) * 8
    c_all = jnp.pad(c_all, ((0, bp - c_all.shape[0]), (0, 0)))
    mods = _ada_call(c_all, ada_w, ada_b)

    ev_w_in_b = jnp.pad(ev_w_in, ((0, 0), (0, 0), (0, EV_IN_PAD - EV_IN))).astype(BF16)
    ev_gw2_b = jnp.pad(ev_gate_w2, ((0, 0), (0, LANES - GATE_RANK), (0, 0))).astype(BF16)
    ev_w_out_b = ev_w_out.astype(BF16)
    od_w_in_b = od_w_in.astype(BF16)
    od_gw_b = od_group_w.astype(BF16)
    od_w_out_b = od_w_out.astype(BF16)

    zero_conv = jnp.zeros((b_p, CONV_PAD, D_A), F32)
    zero_gla = jnp.zeros((b_p, H_B, DK_B, DV_B), F32)
    zero_pool = jnp.zeros((b_p, POOL_PAD, D_C), F32)
    fg = final_g.reshape(1, D_MODEL)

    xp, xs = x_prompt, x_sample
    conv_p, gla_p, pool_p, conv_s, gla_s, pool_s = [], [], [], [], [], []
    for l in range(DEPTH):
        mod_p = mods[l, :, :b_p].reshape(3, b_p, 1, D_MODEL)
        mod_s = mods[l, :, b_p:b_p + b_s].reshape(3, b_s, 1, D_MODEL)
        ng = norm_g[l].reshape(1, D_MODEL)
        if l % 2 == 0:
            e = l // 2
            w = (ng, ev_w_in_b[e], ev_conv_w[e], ev_conv_b[e].reshape(1, D_A),
                 ev_ln_g[e].reshape(1, D_A), ev_ln_b[e].reshape(1, D_A), ev_gw2_b[e],
                 ev_gate_b[e].reshape(1, D_BK), ev_head_g[e].reshape(1, D_BV), ev_w_out_b[e])
            xp, cp, gp = _even_call(xp, mod_p, zero_conv, zero_gla, *w, **PROMPT_TILE)
            xs, cs, gs = _even_call(xs, mod_s, state_conv[e], state_gla[e], *w, **SAMPLE_TILE)
            conv_p.append(cp)
            gla_p.append(gp)
            conv_s.append(cs)
            gla_s.append(gs)
        else:
            o = l // 2
            final = l == DEPTH - 1
            w = (ng, od_w_in_b[o], od_gw_b[o], od_group_b[o].reshape(1, D_C),
                 od_scale[o].reshape(1, D_C), od_w_out_b[o], fg)
            xp, pp = _odd_call(xp, mod_p, zero_pool, *w, pos0=0, final=final, **PROMPT_TILE)
            xs, ps = _odd_call(xs, mod_s, state_pool[o], *w, pos0=PAST_LEN, final=final, **SAMPLE_TILE)
            pool_p.append(pp)
            pool_s.append(ps)
    assert n_even == len(conv_p) and n_odd == len(pool_p)
    return (xp, xs, jnp.stack(conv_p), jnp.stack(gla_p), jnp.stack(pool_p),
            jnp.stack(conv_s), jnp.stack(gla_s), jnp.stack(pool_s))
```

```python
import functools

import numpy as np
import jax
import jax.numpy as jnp
from jax import lax
from jax.experimental import pallas as pl
from jax.experimental.pallas import tpu as pltpu

F32 = jnp.float32
BF16 = jnp.bfloat16

D_MODEL = 1024
DEPTH = 4
EPS = 1e-6
PAST_LEN = 2048

D_A = D_MODEL // 2
CONV_W = 31
CONV_PAD = CONV_W - 1
SEG = 32
SEGS = 8
SEG_PITCH = 68
OUT_PITCH = 36
N_SLAB = D_A // 128
CONV_KB = 4

H_B = 4
DK_B = D_MODEL // 16
DV_B = D_MODEL // 8
D_BK = H_B * DK_B
D_BV = H_B * DV_B
GATE_RANK = 16
GATE_TAU = 16.0
GLA_CHUNK = 64
LANES = 128
GROUP_ROWS = 256

D_C = D_MODEL
POOL_WINDOWS = (2, 4, 8, 16)
POOL_GW = D_C // 4
POOL_PAD = 15
POOL_HIST = 16

O_VAL, O_GLU, O_GATE = 0, D_A, 2 * D_A
O_Q = 3 * D_A
O_K = O_Q + D_BK
O_V = O_K + D_BK
O_BG = O_V + D_BV
O_LR = O_BG + D_BV
EV_IN = O_LR + GATE_RANK
EV_IN_PAD = O_LR + LANES

VMEM_LIMIT = 56 * 1024 * 1024

POOL_RB = 32


def _sigmoid(x):
    return 1.0 / (1.0 + jnp.exp(-x))


def _silu(x):
    return x * _sigmoid(x)


def _split_bf16(x):
    hi = x.astype(BF16)
    lo = (x - hi.astype(F32)).astype(BF16)
    return hi, lo


def _dot(a, b):
    return jnp.dot(a, b, preferred_element_type=F32)


def _dot_nt(a, b):
    return lax.dot_general(a, b, (((1,), (1,)), ((), ())), preferred_element_type=F32)


def _dot_tn(a, b):
    return lax.dot_general(a, b, (((0,), (0,)), ((), ())), preferred_element_type=F32)


def _const_spec(shape):
    nd = len(shape)
    return pl.BlockSpec(shape, lambda *_: (0,) * nd, pipeline_mode=pl.Buffered(1))


def _ada_kernel(c_ref, w_ref, b_ref, o_ref):
    c = c_ref[...]
    cs = _silu(c)
    cs_hi, cs_lo = _split_bf16(cs)
    w_hi, w_lo = _split_bf16(w_ref[0])
    acc = _dot(cs_hi, w_hi) + (_dot(cs_lo, w_hi) + _dot(cs_hi, w_lo))
    o_ref[0, 0] = acc + b_ref[0, 0]


def _ada_call(c_all, ada_w, ada_b):
    bp = c_all.shape[0]
    return pl.pallas_call(
        _ada_kernel,
        grid=(DEPTH, 3),
        in_specs=[
            pl.BlockSpec((bp, D_MODEL), lambda l, j: (0, 0)),
            pl.BlockSpec((1, D_MODEL, D_MODEL), lambda l, j: (l, 0, j)),
            pl.BlockSpec((1, 1, 1, D_MODEL), lambda l, j: (l, j, 0, 0)),
        ],
        out_specs=pl.BlockSpec((1, 1, bp, D_MODEL), lambda l, j: (l, j, 0, 0)),
        out_shape=jax.ShapeDtypeStruct((DEPTH, 3, bp, D_MODEL), F32),
        compiler_params=pltpu.CompilerParams(
            dimension_semantics=("arbitrary", "arbitrary"), vmem_limit_bytes=VMEM_LIMIT),
        name="ada_mod",
    )(c_all, ada_w, ada_b.reshape(DEPTH, 3, 1, D_MODEL))


def _modulated_input(x_ref, mod_ref, ng_ref):
    x3 = x_ref[...]
    nb, tl, d = x3.shape
    shift = mod_ref[0]
    scale = mod_ref[1]
    ms = jnp.mean(x3 * x3, axis=-1, keepdims=True)
    gs = ng_ref[...] * (1.0 + scale)
    h = (x3 * lax.rsqrt(ms + EPS)) * gs + shift
    return h.astype(BF16).reshape(nb * tl, d)


def _even_kernel(x_ref, mod_ref, cst_ref, gst_ref, ng_ref, win_ref, cw_ref, cb_ref, lng_ref,
                 lnb_ref, gw2_ref, gb_ref, hg_ref, wout_ref, tri_ref, ones_ref,
                 xo_ref, co_ref, go_ref,
                 proj_scr, cbuf, obuf, hist_scr, st_scr, o_scr, y_scr, *, nb, tl, chunk):
    t = pl.program_id(1)
    last = pl.num_programs(1) - 1
    m = nb * tl
    gr = min(m, GROUP_ROWS)

    lane = lax.broadcasted_iota(jnp.int32, (1, LANES), 1)
    lo_half = lane < DK_B

    @pl.when(t == 0)
    def _init():
        hist_scr[...] = jnp.zeros_like(hist_scr)
        hist_scr[:, SEG - CONV_PAD:SEG, :] = cst_ref[...]
        for n in range(nb):
            for hh in range(H_B):
                s = gst_ref[n, hh]
                z = jnp.zeros_like(s)
                wide = jnp.concatenate([s, z] if hh % 2 == 0 else [z, s], axis=0)
                st_scr[n, hh] = wide.T

    hb = _modulated_input(x_ref, mod_ref, ng_ref)
    proj_scr[...] = _dot(hb, win_ref[...])

    segs_per_seq = tl // SEG
    prev = None
    for gi in range(m // SEG):
        g, i = divmod(gi, SEGS)
        n, si = divmod(gi, segs_per_seq)
        rows = slice(gi * SEG, (gi + 1) * SEG)
        blk = proj_scr[rows, O_VAL:O_VAL + D_A] * _sigmoid(proj_scr[rows, O_GLU:O_GLU + D_A])
        halo = hist_scr[n] if si == 0 else prev
        for s in range(N_SLAB):
            ls = slice(s * LANES, (s + 1) * LANES)
            cbuf[g, s, i * SEG_PITCH:i * SEG_PITCH + SEG, :] = halo[:, ls]
            cbuf[g, s, i * SEG_PITCH + SEG:i * SEG_PITCH + 2 * SEG, :] = blk[:, ls]
        if si == segs_per_seq - 1:
            hist_scr[n] = blk
        prev = blk

    inv_da = 1.0 / D_A
    for g in range(m // (SEG * SEGS)):
        for kb in range(SEG // CONV_KB):
            acc = [[jnp.broadcast_to(cb_ref[:, s * LANES:(s + 1) * LANES], (SEGS, LANES))
                    for s in range(N_SLAB)] for _ in range(CONV_KB)]
            for j in range(CONV_W):
                for s in range(N_SLAB):
                    wj = cw_ref[j:j + 1, s * LANES:(s + 1) * LANES]
                    for kk in range(CONV_KB):
                        start = SEG - CONV_PAD + kb * CONV_KB + kk + j
                        acc[kk][s] = acc[kk][s] + wj * cbuf[g, s, pl.ds(start, SEGS, stride=SEG_PITCH), :]
            for kk in range(CONV_KB):
                a = acc[kk]
                mu = jnp.sum(a[0] + a[1] + a[2] + a[3], axis=-1, keepdims=True) * inv_da
                xc = [v - mu for v in a]
                sq = xc[0] * xc[0] + xc[1] * xc[1] + xc[2] * xc[2] + xc[3] * xc[3]
                rstd = lax.rsqrt(jnp.sum(sq, axis=-1, keepdims=True) * inv_da + EPS)
                for s in range(N_SLAB):
                    ls = slice(s * LANES, (s + 1) * LANES)
                    yn = xc[s] * rstd * lng_ref[:, ls] + lnb_ref[:, ls]
                    obuf[g, s, pl.ds(kb * CONV_KB + kk, SEGS, stride=OUT_PITCH), :] = _silu(yn)

    for gi in range(m // SEG):
        g, i = divmod(gi, SEGS)
        rows = slice(gi * SEG, (gi + 1) * SEG)
        for s in range(N_SLAB):
            ya = obuf[g, s, i * OUT_PITCH:i * OUT_PITCH + SEG, :] * _silu(
                proj_scr[rows, O_GATE + s * LANES:O_GATE + (s + 1) * LANES])
            y_scr[rows, s * LANES:(s + 1) * LANES] = ya.astype(BF16)

    pre = _dot(proj_scr[:, O_LR:O_LR + LANES].astype(BF16), gw2_ref[...]) + gb_ref[...]
    la = (jnp.minimum(pre, 0.0) - jnp.log1p(jnp.exp(-jnp.abs(pre)))) * (1.0 / GATE_TAU)
    la_hi, la_lo = _split_bf16(la)
    tri = tri_ref[...]
    ones_bd = ones_ref[...]
    ri = lax.broadcasted_iota(jnp.int32, (chunk, chunk), 0)
    ci = lax.broadcasted_iota(jnp.int32, (chunk, chunk), 1)
    causal = ri >= ci
    pair_lo = jnp.concatenate([lo_half, lo_half], axis=1)
    for g in range(m // gr):
        rows = slice(g * gr, (g + 1) * gr)
        b = _dot(tri, la_hi[rows]) + _dot(tri, la_lo[rows])
        bl = _dot(ones_bd, la_hi[rows]) + _dot(ones_bd, la_lo[rows])
        k = proj_scr[rows, O_K:O_K + D_BK]
        q_e = proj_scr[rows, O_Q:O_Q + D_BK] * (DK_B ** -0.5) * jnp.exp(b)
        q_even = jnp.where(pair_lo, q_e, 0.0).astype(BF16)
        q_odd = jnp.where(pair_lo, 0.0, q_e).astype(BF16)
        k_e = (k * jnp.exp(-b)).astype(BF16)
        k_l = (k * jnp.exp(bl - b)).astype(BF16)
        dec = jnp.exp(bl)
        vb = proj_scr[rows, O_V:O_V + D_BV].astype(BF16)
        for c in range(gr // chunk):
            cr = slice(c * chunk, (c + 1) * chunk)
            row0 = g * gr + c * chunk
            n = row0 // tl
            for hh in range(H_B):
                pc = slice((hh // 2) * LANES, (hh // 2 + 1) * LANES)
                vc = slice(hh * DV_B, (hh + 1) * DV_B)
                qh = (q_even if hh % 2 == 0 else q_odd)[cr, pc]
                att = _dot_nt(qh, k_e[cr, pc])
                att = jnp.where(causal, att, 0.0).astype(BF16)
                s_t = st_scr[n, hh]
                o = _dot(att, vb[cr, vc]) + _dot_nt(qh, s_t.astype(BF16))
                o_scr[row0:row0 + chunk, vc] = o
                upd = _dot_tn(vb[cr, vc], k_l[cr, pc])
                keep = lo_half if hh % 2 == 0 else jnp.logical_not(lo_half)
                st_scr[n, hh] = dec[c * chunk:c * chunk + 1, pc] * s_t + jnp.where(keep, upd, 0.0)

    for hh in range(H_B):
        vc = slice(hh * DV_B, (hh + 1) * DV_B)
        oh = o_scr[:, vc]
        ms = jnp.mean(oh * oh, axis=-1, keepdims=True)
        on = oh * lax.rsqrt(ms + EPS) * hg_ref[:, vc]
        yb = on * _silu(proj_scr[:, O_BG + hh * DV_B:O_BG + (hh + 1) * DV_B])
        y_scr[:, D_A + hh * DV_B:D_A + (hh + 1) * DV_B] = yb.astype(BF16)

    y = _dot(y_scr[...], wout_ref[...])
    xo_ref[...] = x_ref[...] + mod_ref[2] * y.reshape(nb, tl, D_MODEL)

    @pl.when(t == last)
    def _fin():
        co_ref[...] = hist_scr[:, SEG - CONV_PAD:SEG, :]
        for n in range(nb):
            for hh in range(H_B):
                s = st_scr[n, hh].T
                off = (hh % 2) * DK_B
                go_ref[n, hh] = s[off:off + DK_B, :]


def _decay_matrices(rows, chunk):
    i = np.arange(rows)
    same = (i[:, None] // chunk) == (i[None, :] // chunk)
    tri = same & (i[:, None] >= i[None, :])
    return jnp.asarray(tri, BF16), jnp.asarray(same, BF16)


def _even_call(x, mod, conv_st, gla_st, ng, w_in, cw, cb, lng, lnb, gw2, gb, hg, w_out, *, nb, tl):
    b, l, d = x.shape
    chunk = min(GLA_CHUNK, l)
    m = nb * tl
    gr = min(m, GROUP_ROWS)
    assert b % nb == 0 and l % tl == 0 and tl % chunk == 0 and m % gr == 0 and gr % chunk == 0
    assert tl % SEG == 0 and m % (SEG * SEGS) == 0
    tri, ones_bd = _decay_matrices(gr, chunk)
    kern = functools.partial(_even_kernel, nb=nb, tl=tl, chunk=chunk)
    return pl.pallas_call(
        kern,
        grid=(b // nb, l // tl),
        in_specs=[
            pl.BlockSpec((nb, tl, d), lambda i, t: (i, t, 0)),
            pl.BlockSpec((3, nb, 1, d), lambda i, t: (0, i, 0, 0)),
            pl.BlockSpec((nb, CONV_PAD, D_A), lambda i, t: (i, 0, 0)),
            pl.BlockSpec((nb, H_B, DK_B, DV_B), lambda i, t: (i, 0, 0, 0)),
            _const_spec((1, d)),
            _const_spec((d, EV_IN_PAD)),
            _const_spec((CONV_W, D_A)),
            _const_spec((1, D_A)),
            _const_spec((1, D_A)),
            _const_spec((1, D_A)),
            _const_spec((LANES, D_BK)),
            _const_spec((1, D_BK)),
            _const_spec((1, D_BV)),
            _const_spec((D_A + D_BV, d)),
            _const_spec((gr, gr)),
            _const_spec((gr, gr)),
        ],
        out_specs=[
            pl.BlockSpec((nb, tl, d), lambda i, t: (i, t, 0)),
            pl.BlockSpec((nb, CONV_PAD, D_A), lambda i, t: (i, 0, 0)),
            pl.BlockSpec((nb, H_B, DK_B, DV_B), lambda i, t: (i, 0, 0, 0)),
        ],
        out_shape=[
            jax.ShapeDtypeStruct((b, l, d), F32),
            jax.ShapeDtypeStruct((b, CONV_PAD, D_A), F32),
            jax.ShapeDtypeStruct((b, H_B, DK_B, DV_B), F32),
        ],
        scratch_shapes=[
            pltpu.VMEM((m, EV_IN_PAD), F32),
            pltpu.VMEM((m // (SEG * SEGS), N_SLAB, SEGS * SEG_PITCH, LANES), F32),
            pltpu.VMEM((m // (SEG * SEGS), N_SLAB, SEGS * OUT_PITCH, LANES), F32),
            pltpu.VMEM((nb, SEG, D_A), F32),
            pltpu.VMEM((nb, H_B, DV_B, LANES), F32),
            pltpu.VMEM((m, D_BV), F32),
            pltpu.VMEM((m, D_A + D_BV), BF16),
        ],
        compiler_params=pltpu.CompilerParams(
            dimension_semantics=("arbitrary", "arbitrary"), vmem_limit_bytes=VMEM_LIMIT),
        name="even_layer",
    )(x, mod, conv_st, gla_st, ng, w_in, cw, cb, lng, lnb, gw2, gb, hg, w_out, tri, ones_bd)


def _odd_kernel(x_ref, mod_ref, pst_ref, ng_ref, win_ref, gw_ref, gb_ref, sc_ref, wout_ref, fg_ref,
                xo_ref, po_ref,
                proj_scr, vp_scr, p_scr, z_scr, *, nb, tl, pos0, final):
    t = pl.program_id(1)
    last = pl.num_programs(1) - 1

    @pl.when(t == 0)
    def _init():
        vp_scr[:, POOL_HIST - POOL_PAD:POOL_HIST, :] = pst_ref[...]

    hb = _modulated_input(x_ref, mod_ref, ng_ref)
    proj_scr[...] = _dot(hb, win_ref[...])
    vp_scr[:, POOL_HIST:POOL_HIST + tl, :] = proj_scr[:, 0:D_C].reshape(nb, tl, D_C)

    row = lax.broadcasted_iota(jnp.int32, (POOL_RB, POOL_GW), 0)
    for n in range(nb):
        for rb in range(tl // POOL_RB):
            r0 = rb * POOL_RB
            pos1 = row + (pos0 + 1 + r0) + t * tl
            for gi, w in enumerate(POOL_WINDOWS):
                cs = slice(gi * POOL_GW, (gi + 1) * POOL_GW)
                cur = vp_scr[n, POOL_HIST + r0:POOL_HIST + r0 + POOL_RB, cs]
                acc = cur
                for i in range(1, w):
                    acc = acc + vp_scr[n, POOL_HIST + r0 - i:POOL_HIST + r0 - i + POOL_RB, cs]
                cnt = jnp.minimum(pos1, w).astype(F32)
                p = acc / cnt - cur
                p_scr[n * tl + r0:n * tl + r0 + POOL_RB, cs] = p.astype(BF16)
    vp_scr[:, 0:POOL_HIST, :] = vp_scr[:, tl:tl + POOL_HIST, :]

    for gi in range(len(POOL_WINDOWS)):
        cs = slice(gi * POOL_GW, (gi + 1) * POOL_GW)
        pg = _dot(p_scr[:, cs], gw_ref[gi]) + gb_ref[:, cs]
        z = pg * sc_ref[:, cs] * _silu(proj_scr[:, D_C + gi * POOL_GW:D_C + (gi + 1) * POOL_GW])
        z_scr[:, cs] = z.astype(BF16)

    y = _dot(z_scr[...], wout_ref[...])
    xo = x_ref[...] + mod_ref[2] * y.reshape(nb, tl, D_MODEL)
    if final:
        ms = jnp.mean(xo * xo, axis=-1, keepdims=True)
        xo = xo * lax.rsqrt(ms + EPS) * fg_ref[...]
    xo_ref[...] = xo

    @pl.when(t == last)
    def _fin():
        po_ref[...] = vp_scr[:, POOL_HIST - POOL_PAD:POOL_HIST, :]


def _odd_call(x, mod, pool_st, ng, w_in, gw, gb, sc, w_out, fg, *, nb, tl, pos0, final):
    b, l, d = x.shape
    m = nb * tl
    assert b % nb == 0 and l % tl == 0 and tl >= POOL_HIST and tl % POOL_RB == 0
    kern = functools.partial(_odd_kernel, nb=nb, tl=tl, pos0=pos0, final=final)
    return pl.pallas_call(
        kern,
        grid=(b // nb, l // tl),
        in_specs=[
            pl.BlockSpec((nb, tl, d), lambda i, t: (i, t, 0)),
            pl.BlockSpec((3, nb, 1, d), lambda i, t: (0, i, 0, 0)),
            pl.BlockSpec((nb, POOL_PAD, D_C), lambda i, t: (i, 0, 0)),
            _const_spec((1, d)),
            _const_spec((d, 2 * D_C)),
            _const_spec((len(POOL_WINDOWS), POOL_GW, POOL_GW)),
            _const_spec((1, D_C)),
            _const_spec((1, D_C)),
            _const_spec((D_C, d)),
            _const_spec((1, d)),
        ],
        out_specs=[
            pl.BlockSpec((nb, tl, d), lambda i, t: (i, t, 0)),
            pl.BlockSpec((nb, POOL_PAD, D_C), lambda i, t: (i, 0, 0)),
        ],
        out_shape=[
            jax.ShapeDtypeStruct((b, l, d), F32),
            jax.ShapeDtypeStruct((b, POOL_PAD, D_C), F32),
        ],
        scratch_shapes=[
            pltpu.VMEM((m, 2 * D_C), F32),
            pltpu.VMEM((nb, POOL_HIST + tl, D_C), F32),
            pltpu.VMEM((m, D_C), BF16),
            pltpu.VMEM((m, D_C), BF16),
        ],
        compiler_params=pltpu.CompilerParams(
            dimension_semantics=("arbitrary", "arbitrary"), vmem_limit_bytes=VMEM_LIMIT),
        name="odd_layer",
    )(x, mod, pool_st, ng, w_in, gw, gb, sc, w_out, fg)


PROMPT_TILE = dict(nb=2, tl=256)
SAMPLE_TILE = dict(nb=8, tl=32)


def kernel(x_prompt, x_sample, c_prompt, c_sample, state_conv, state_gla, state_pool, norm_g, ada_w, ada_b, ev_w_in, ev_conv_w, ev_conv_b, ev_ln_g, ev_ln_b, ev_gate_w2, ev_gate_b, ev_head_g, ev_w_out, od_w_in, od_group_w, od_group_b, od_scale, od_w_out, final_g):
    b_p = x_prompt.shape[0]
    b_s = x_sample.shape[0]
    n_even = ev_w_in.shape[0]
    n_odd = od_w_in.shape[0]

    c_all = jnp.concatenate([c_prompt, c_sample], axis=0)
    bp = -(-c_all.shape[0] // 8) * 8
    c_all = jnp.pad(c_all, ((0, bp - c_all.shape[0]), (0, 0)))
    mods = _ada_call(c_all, ada_w, ada_b)

    ev_w_in_b = jnp.pad(ev_w_in, ((0, 0), (0, 0), (0, EV_IN_PAD - EV_IN))).astype(BF16)
    ev_gw2_b = jnp.pad(ev_gate_w2, ((0, 0), (0, LANES - GATE_RANK), (0, 0))).astype(BF16)
    ev_w_out_b = ev_w_out.astype(BF16)
    od_w_in_b = od_w_in.astype(BF16)
    od_gw_b = od_group_w.astype(BF16)
    od_w_out_b = od_w_out.astype(BF16)

    zero_conv = jnp.zeros((b_p, CONV_PAD, D_A), F32)
    zero_gla = jnp.zeros((b_p, H_B, DK_B, DV_B), F32)
    zero_pool = jnp.zeros((b_p, POOL_PAD, D_C), F32)
    fg = final_g.reshape(1, D_MODEL)

    xp, xs = x_prompt, x_sample
    conv_p, gla_p, pool_p, conv_s, gla_s, pool_s = [], [], [], [], [], []
    for l in range(DEPTH):
        mod_p = mods[l, :, :b_p].reshape(3, b_p, 1, D_MODEL)
        mod_s = mods[l, :, b_p:b_p + b_s].reshape(3, b_s, 1, D_MODEL)
        ng = norm_g[l].reshape(1, D_MODEL)
        if l % 2 == 0:
            e = l // 2
            w = (ng, ev_w_in_b[e], ev_conv_w[e], ev_conv_b[e].reshape(1, D_A),
                 ev_ln_g[e].reshape(1, D_A), ev_ln_b[e].reshape(1, D_A), ev_gw2_b[e],
                 ev_gate_b[e].reshape(1, D_BK), ev_head_g[e].reshape(1, D_BV), ev_w_out_b[e])
            xp, cp, gp = _even_call(xp, mod_p, zero_conv, zero_gla, *w, **PROMPT_TILE)
            xs, cs, gs = _even_call(xs, mod_s, state_conv[e], state_gla[e], *w, **SAMPLE_TILE)
            conv_p.append(cp)
            gla_p.append(gp)
            conv_s.append(cs)
            gla_s.append(gs)
        else:
            o = l // 2
            final = l == DEPTH - 1
            w = (ng, od_w_in_b[o], od_gw_b[o], od_group_b[o].reshape(1, D_C),
                 od_scale[o].reshape(1, D_C), od_w_out_b[o], fg)
            xp, pp = _odd_call(xp, mod_p, zero_pool, *w, pos0=0, final=final, **PROMPT_TILE)
            xs, ps = _odd_call(xs, mod_s, state_pool[o], *w, pos0=PAST_LEN, final=final, **SAMPLE_TILE)
            pool_p.append(pp)
            pool_s.append(ps)
    assert n_even == len(conv_p) and n_odd == len(pool_p)
    return (xp, xs, jnp.stack(conv_p), jnp.stack(gla_p), jnp.stack(pool_p),
            jnp.stack(conv_s), jnp.stack(gla_s), jnp.stack(pool_s))
```

```python
import functools

import numpy as np
import jax
import jax.numpy as jnp
from jax import lax
from jax.experimental import pallas as pl
from jax.experimental.pallas import tpu as pltpu

F32 = jnp.float32
BF16 = jnp.bfloat16

LANES = 128
SUBLANES = 8
VMEM_LIMIT = 56 * 1024 * 1024
MXU_COLS = 256
GROUP_ROWS = MXU_COLS

D_MODEL = 1024
DEPTH = 4
EPS = 1e-6
PAST_LEN = 2048
LOG2E = 1.4426950408889634

D_A = D_MODEL // 2
CONV_W = 31
CONV_PAD = CONV_W - 1
SEG = 32
SEGS = SUBLANES
SEG_PITCH = 68
OUT_PITCH = 36
N_SLAB = D_A // LANES
CONV_KB = 4

H_B = 4
DK_B = D_MODEL // 16
DV_B = D_MODEL // 8
D_BK = H_B * DK_B
D_BV = H_B * DV_B
GATE_RANK = 16
GATE_TAU = 16.0
GLA_CHUNK = 64

D_C = D_MODEL
POOL_WINDOWS = (2, 4, 8, 16)
POOL_GW = D_C // len(POOL_WINDOWS)
POOL_PAD = 15
POOL_HIST = 16
POOL_PITCH = 52
P_SLAB = D_C // LANES

O_VAL, O_GLU, O_GATE = 0, D_A, 2 * D_A
O_Q = 3 * D_A
O_K = O_Q + D_BK
O_V = O_K + D_BK
O_BG = O_V + D_BV
O_LR = O_BG + D_BV
EV_IN = O_LR + GATE_RANK
EV_IN_PAD = O_LR + LANES


def _sigmoid(x):
    return 1.0 / (1.0 + jnp.exp2(x * (-LOG2E)))


def _silu(x):
    return x * _sigmoid(x)


def _split_bf16(x):
    hi = x.astype(BF16)
    lo = (x - hi.astype(F32)).astype(BF16)
    return hi, lo


def _dot(a, b):
    return jnp.dot(a, b, preferred_element_type=F32)


def _dot_nt(a, b):
    return lax.dot_general(a, b, (((1,), (1,)), ((), ())), preferred_element_type=F32)


def _dot_tn(a, b):
    return lax.dot_general(a, b, (((0,), (0,)), ((), ())), preferred_element_type=F32)


def _const_spec(shape):
    nd = len(shape)
    return pl.BlockSpec(shape, lambda *_: (0,) * nd, pipeline_mode=pl.Buffered(1))


def _ada_kernel(c_ref, w_ref, b_ref, o_ref):
    c = c_ref[...]
    cs = _silu(c)
    cs_hi, cs_lo = _split_bf16(cs)
    w_hi, w_lo = _split_bf16(w_ref[0])
    acc = _dot(cs_hi, w_hi) + (_dot(cs_lo, w_hi) + _dot(cs_hi, w_lo))
    o_ref[0, 0] = acc + b_ref[0, 0]


def _ada_call(c_all, ada_w, ada_b):
    bp = c_all.shape[0]
    return pl.pallas_call(
        _ada_kernel,
        grid=(DEPTH, 3),
        in_specs=[
            pl.BlockSpec((bp, D_MODEL), lambda l, j: (0, 0)),
            pl.BlockSpec((1, D_MODEL, D_MODEL), lambda l, j: (l, 0, j)),
            pl.BlockSpec((1, 1, 1, D_MODEL), lambda l, j: (l, j, 0, 0)),
        ],
        out_specs=pl.BlockSpec((1, 1, bp, D_MODEL), lambda l, j: (l, j, 0, 0)),
        out_shape=jax.ShapeDtypeStruct((DEPTH, 3, bp, D_MODEL), F32),
        compiler_params=pltpu.CompilerParams(
            dimension_semantics=("arbitrary", "arbitrary"), vmem_limit_bytes=VMEM_LIMIT),
        name="ada_mod",
    )(c_all, ada_w, ada_b.reshape(DEPTH, 3, 1, D_MODEL))


def _modulated_input(x3, mod_ref, ng_ref):
    nb, tl, d = x3.shape
    shift = mod_ref[0]
    scale = mod_ref[1]
    ms = jnp.mean(x3 * x3, axis=-1, keepdims=True)
    gs = ng_ref[...] * (1.0 + scale)
    h = (x3 * lax.rsqrt(ms + EPS)) * gs + shift
    return h.astype(BF16).reshape(nb * tl, d)


def _run_tiles(x_ref, mod_ref, ng_ref, win_ref, proj_scr, mixer, *, tl, tps):
    for j in range(tps):
        hb = _modulated_input(x_ref[:, j * tl:(j + 1) * tl, :], mod_ref, ng_ref)
        proj_scr[j % 2] = _dot(hb, win_ref[...])
        mixer(proj_scr.at[j % 2], j)


def _even_mixer(proj, j, x_ref, mod_ref, cw_ref, cb_ref, lng_ref, lnb_ref, gw2_ref, gb_ref,
                hg_ref, wout_ref, tri_ref, ones_ref, xo_ref,
                cbuf, obuf, hist_scr, st_scr, o_scr, y_scr, *, nb, tl, chunk):
    m = nb * tl
    gr = min(m, GROUP_ROWS)
    lane = lax.broadcasted_iota(jnp.int32, (1, LANES), 1)
    lo_half = lane < DK_B

    segs_per_seq = tl // SEG
    prev = None
    for gi in range(m // SEG):
        g, i = divmod(gi, SEGS)
        n, si = divmod(gi, segs_per_seq)
        rows = slice(gi * SEG, (gi + 1) * SEG)
        blk = proj[rows, O_VAL:O_VAL + D_A] * _sigmoid(proj[rows, O_GLU:O_GLU + D_A])
        halo = hist_scr[n] if si == 0 else prev
        for s in range(N_SLAB):
            ls = slice(s * LANES, (s + 1) * LANES)
            cbuf[g, s, i * SEG_PITCH:i * SEG_PITCH + SEG, :] = halo[:, ls]
            cbuf[g, s, i * SEG_PITCH + SEG:i * SEG_PITCH + 2 * SEG, :] = blk[:, ls]
        if si == segs_per_seq - 1:
            hist_scr[n] = blk
        prev = blk

    inv_da = 1.0 / D_A
    for g in range(m // (SEG * SEGS)):
        for kb in range(SEG // CONV_KB):
            acc = [[jnp.broadcast_to(cb_ref[:, s * LANES:(s + 1) * LANES], (SEGS, LANES))
                    for s in range(N_SLAB)] for _ in range(CONV_KB)]
            for jt in range(CONV_W):
                for s in range(N_SLAB):
                    wj = cw_ref[jt:jt + 1, s * LANES:(s + 1) * LANES]
                    for kk in range(CONV_KB):
                        start = SEG - CONV_PAD + kb * CONV_KB + kk + jt
                        acc[kk][s] = acc[kk][s] + wj * cbuf[g, s, pl.ds(start, SEGS, stride=SEG_PITCH), :]
            for kk in range(CONV_KB):
                a = acc[kk]
                mu = jnp.sum(a[0] + a[1] + a[2] + a[3], axis=-1, keepdims=True) * inv_da
                xc = [v - mu for v in a]
                sq = xc[0] * xc[0] + xc[1] * xc[1] + xc[2] * xc[2] + xc[3] * xc[3]
                rstd = lax.rsqrt(jnp.sum(sq, axis=-1, keepdims=True) * inv_da + EPS)
                for s in range(N_SLAB):
                    ls = slice(s * LANES, (s + 1) * LANES)
                    yn = xc[s] * rstd * lng_ref[:, ls] + lnb_ref[:, ls]
                    obuf[g, s, pl.ds(kb * CONV_KB + kk, SEGS, stride=OUT_PITCH), :] = _silu(yn)

    for gi in range(m // SEG):
        g, i = divmod(gi, SEGS)
        rows = slice(gi * SEG, (gi + 1) * SEG)
        for s in range(N_SLAB):
            ya = obuf[g, s, i * OUT_PITCH:i * OUT_PITCH + SEG, :] * _silu(
                proj[rows, O_GATE + s * LANES:O_GATE + (s + 1) * LANES])
            y_scr[rows, s * LANES:(s + 1) * LANES] = ya.astype(BF16)

    pre = _dot(proj[:, O_LR:O_LR + LANES].astype(BF16), gw2_ref[...]) + gb_ref[...]
    la = (jnp.minimum(pre, 0.0) - jnp.log1p(jnp.exp(-jnp.abs(pre)))) * (1.0 / GATE_TAU)
    la_hi, la_lo = _split_bf16(la)
    tri = tri_ref[...]
    ones_bd = ones_ref[...]
    ri = lax.broadcasted_iota(jnp.int32, (chunk, chunk), 0)
    ci = lax.broadcasted_iota(jnp.int32, (chunk, chunk), 1)
    causal = ri >= ci
    pair_lo = jnp.concatenate([lo_half, lo_half], axis=1)
    for g in range(m // gr):
        rows = slice(g * gr, (g + 1) * gr)
        b = _dot(tri, la_hi[rows]) + _dot(tri, la_lo[rows])
        bl = _dot(ones_bd, la_hi[rows]) + _dot(ones_bd, la_lo[rows])
        k = proj[rows, O_K:O_K + D_BK]
        q_e = proj[rows, O_Q:O_Q + D_BK] * (DK_B ** -0.5) * jnp.exp(b)
        q_even = jnp.where(pair_lo, q_e, 0.0).astype(BF16)
        q_odd = jnp.where(pair_lo, 0.0, q_e).astype(BF16)
        k_e = (k * jnp.exp(-b)).astype(BF16)
        k_l = (k * jnp.exp(bl - b)).astype(BF16)
        dec = jnp.exp(bl)
        vb = proj[rows, O_V:O_V + D_BV].astype(BF16)
        for c in range(gr // chunk):
            cr = slice(c * chunk, (c + 1) * chunk)
            row0 = g * gr + c * chunk
            n = row0 // tl
            for hh in range(H_B):
                pc = slice((hh // 2) * LANES, (hh // 2 + 1) * LANES)
                vc = slice(hh * DV_B, (hh + 1) * DV_B)
                qh = (q_even if hh % 2 == 0 else q_odd)[cr, pc]
                att = _dot_nt(qh, k_e[cr, pc])
                att = jnp.where(causal, att, 0.0).astype(BF16)
                s_t = st_scr[n, hh]
                o = _dot(att, vb[cr, vc]) + _dot_nt(qh, s_t.astype(BF16))
                o_scr[row0:row0 + chunk, vc] = o
                upd = _dot_tn(vb[cr, vc], k_l[cr, pc])
                keep = lo_half if hh % 2 == 0 else jnp.logical_not(lo_half)
                st_scr[n, hh] = dec[c * chunk:c * chunk + 1, pc] * s_t + jnp.where(keep, upd, 0.0)

    for hh in range(H_B):
        vc = slice(hh * DV_B, (hh + 1) * DV_B)
        oh = o_scr[:, vc]
        ms = jnp.mean(oh * oh, axis=-1, keepdims=True)
        on = oh * lax.rsqrt(ms + EPS) * hg_ref[:, vc]
        yb = on * _silu(proj[:, O_BG + hh * DV_B:O_BG + (hh + 1) * DV_B])
        y_scr[:, D_A + hh * DV_B:D_A + (hh + 1) * DV_B] = yb.astype(BF16)

    y = _dot(y_scr[...], wout_ref[...])
    tr = slice(j * tl, (j + 1) * tl)
    xo_ref[:, tr, :] = x_ref[:, tr, :] + mod_ref[2] * y.reshape(nb, tl, D_MODEL)


def _even_kernel(x_ref, mod_ref, cst_ref, gst_ref, ng_ref, win_ref, cw_ref, cb_ref, lng_ref,
                 lnb_ref, gw2_ref, gb_ref, hg_ref, wout_ref, tri_ref, ones_ref,
                 xo_ref, co_ref, go_ref,
                 proj_scr, cbuf, obuf, hist_scr, st_scr, o_scr, y_scr, *, nb, tl, tps, chunk):
    step = pl.program_id(1)
    last = pl.num_programs(1) - 1

    @pl.when(step == 0)
    def _init():
        hist_scr[...] = jnp.zeros_like(hist_scr)
        hist_scr[:, SEG - CONV_PAD:SEG, :] = cst_ref[...]
        for n in range(nb):
            for hh in range(H_B):
                s = gst_ref[n, hh]
                z = jnp.zeros_like(s)
                wide = jnp.concatenate([s, z] if hh % 2 == 0 else [z, s], axis=0)
                st_scr[n, hh] = wide.T

    mixer = functools.partial(
        _even_mixer, x_ref=x_ref, mod_ref=mod_ref, cw_ref=cw_ref, cb_ref=cb_ref, lng_ref=lng_ref,
        lnb_ref=lnb_ref, gw2_ref=gw2_ref, gb_ref=gb_ref, hg_ref=hg_ref, wout_ref=wout_ref,
        tri_ref=tri_ref, ones_ref=ones_ref, xo_ref=xo_ref, cbuf=cbuf, obuf=obuf,
        hist_scr=hist_scr, st_scr=st_scr, o_scr=o_scr, y_scr=y_scr, nb=nb, tl=tl, chunk=chunk)
    _run_tiles(x_ref, mod_ref, ng_ref, win_ref, proj_scr, mixer, tl=tl, tps=tps)

    @pl.when(step == last)
    def _fin():
        co_ref[...] = hist_scr[:, SEG - CONV_PAD:SEG, :]
        for n in range(nb):
            for hh in range(H_B):
                s = st_scr[n, hh].T
                off = (hh % 2) * DK_B
                go_ref[n, hh] = s[off:off + DK_B, :]


def _decay_matrices(rows, chunk):
    i = np.arange(rows)
    same = (i[:, None] // chunk) == (i[None, :] // chunk)
    tri = same & (i[:, None] >= i[None, :])
    return jnp.asarray(tri, BF16), jnp.asarray(same, BF16)


def _even_call(x, mod, conv_st, gla_st, ng, w_in, cw, cb, lng, lnb, gw2, gb, hg, w_out, *, nb, tl, tps):
    b, l, d = x.shape
    chunk = min(GLA_CHUNK, l)
    m = nb * tl
    gr = min(m, GROUP_ROWS)
    n_tiles = l // tl
    assert b % nb == 0 and l % (tl * tps) == 0 and tl % chunk == 0 and m % gr == 0 and gr % chunk == 0
    assert tl % SEG == 0 and m % (SEG * SEGS) == 0
    tri, ones_bd = _decay_matrices(gr, chunk)
    kern = functools.partial(_even_kernel, nb=nb, tl=tl, tps=tps, chunk=chunk)
    groups = m // (SEG * SEGS)
    return pl.pallas_call(
        kern,
        grid=(b // nb, n_tiles // tps),
        in_specs=[
            pl.BlockSpec((nb, tps * tl, d), lambda i, s: (i, s, 0)),
            pl.BlockSpec((3, nb, 1, d), lambda i, s: (0, i, 0, 0)),
            pl.BlockSpec((nb, CONV_PAD, D_A), lambda i, s: (i, 0, 0)),
            pl.BlockSpec((nb, H_B, DK_B, DV_B), lambda i, s: (i, 0, 0, 0)),
            _const_spec((1, d)),
            _const_spec((d, EV_IN_PAD)),
            _const_spec((CONV_W, D_A)),
            _const_spec((1, D_A)),
            _const_spec((1, D_A)),
            _const_spec((1, D_A)),
            _const_spec((LANES, D_BK)),
            _const_spec((1, D_BK)),
            _const_spec((1, D_BV)),
            _const_spec((D_A + D_BV, d)),
            _const_spec((gr, gr)),
            _const_spec((gr, gr)),
        ],
        out_specs=[
            pl.BlockSpec((nb, tps * tl, d), lambda i, s: (i, s, 0)),
            pl.BlockSpec((nb, CONV_PAD, D_A), lambda i, s: (i, 0, 0)),
            pl.BlockSpec((nb, H_B, DK_B, DV_B), lambda i, s: (i, 0, 0, 0)),
        ],
        out_shape=[
            jax.ShapeDtypeStruct((b, l, d), F32),
            jax.ShapeDtypeStruct((b, CONV_PAD, D_A), F32),
            jax.ShapeDtypeStruct((b, H_B, DK_B, DV_B), F32),
        ],
        scratch_shapes=[
            pltpu.VMEM((min(tps, 2), m, EV_IN_PAD), F32),
            pltpu.VMEM((groups, N_SLAB, SEGS * SEG_PITCH, LANES), F32),
            pltpu.VMEM((groups, N_SLAB, SEGS * OUT_PITCH, LANES), F32),
            pltpu.VMEM((nb, SEG, D_A), F32),
            pltpu.VMEM((nb, H_B, DV_B, LANES), F32),
            pltpu.VMEM((m, D_BV), F32),
            pltpu.VMEM((m, D_A + D_BV), BF16),
        ],
        compiler_params=pltpu.CompilerParams(
            dimension_semantics=("arbitrary", "arbitrary"), vmem_limit_bytes=VMEM_LIMIT),
        name="even_layer",
    )(x, mod, conv_st, gla_st, ng, w_in, cw, cb, lng, lnb, gw2, gb, hg, w_out, tri, ones_bd)


def _odd_mixer(proj, j, x_ref, mod_ref, gw_ref, gb_ref, sc_ref, wout_ref, fg_ref, xo_ref,
               vbuf, pbuf, phist_scr, p_scr, z_scr, *, nb, tl, tps, pos0, final):
    m = nb * tl
    tile_pos = pos0 + (pl.program_id(1) * tps + j) * tl

    segs_per_seq = tl // SEG
    prev = None
    for gi in range(m // SEG):
        g, i = divmod(gi, SEGS)
        n, si = divmod(gi, segs_per_seq)
        blk = proj[gi * SEG:(gi + 1) * SEG, 0:D_C]
        halo = phist_scr[n] if si == 0 else prev[SEG - POOL_HIST:SEG, :]
        for s in range(P_SLAB):
            ls = slice(s * LANES, (s + 1) * LANES)
            vbuf[g, s, i * POOL_PITCH:i * POOL_PITCH + POOL_HIST, :] = halo[:, ls]
            vbuf[g, s, i * POOL_PITCH + POOL_HIST:i * POOL_PITCH + POOL_HIST + SEG, :] = blk[:, ls]
        if si == segs_per_seq - 1:
            phist_scr[n] = blk[SEG - POOL_HIST:SEG, :]
        prev = blk

    sub = lax.broadcasted_iota(jnp.int32, (SEGS, LANES), 0)
    for g in range(m // (SEG * SEGS)):
        seg_pos1 = tile_pos + 1 + ((sub + g * SEGS) % segs_per_seq) * SEG
        for wi, w in enumerate(POOL_WINDOWS):
            for s in range(wi * (POOL_GW // LANES), (wi + 1) * (POOL_GW // LANES)):
                lo = -(w - 1)
                cur = {k: vbuf[g, s, pl.ds(POOL_HIST + k, SEGS, stride=POOL_PITCH), :]
                       for k in range(lo, SEG)}
                xk = cur
                d = 1
                while d < w:
                    lo += d
                    cur = {k: cur[k] + cur[k - d] for k in range(lo, SEG)}
                    d *= 2
                for k in range(SEG):
                    if k >= w - 1:
                        p = cur[k] * (1.0 / w) - xk[k]
                    else:
                        cnt = jnp.minimum(seg_pos1 + k, w).astype(F32)
                        p = cur[k] / cnt - xk[k]
                    pbuf[g, s, pl.ds(k, SEGS, stride=OUT_PITCH), :] = p

    for gi in range(m // SEG):
        g, i = divmod(gi, SEGS)
        rows = slice(gi * SEG, (gi + 1) * SEG)
        for s in range(P_SLAB):
            p_scr[rows, s * LANES:(s + 1) * LANES] = pbuf[
                g, s, i * OUT_PITCH:i * OUT_PITCH + SEG, :].astype(BF16)

    for gi in range(len(POOL_WINDOWS)):
        cs = slice(gi * POOL_GW, (gi + 1) * POOL_GW)
        pg = _dot(p_scr[:, cs], gw_ref[gi]) + gb_ref[:, cs]
        z = pg * sc_ref[:, cs] * _silu(proj[:, D_C + gi * POOL_GW:D_C + (gi + 1) * POOL_GW])
        z_scr[:, cs] = z.astype(BF16)

    y = _dot(z_scr[...], wout_ref[...])
    tr = slice(j * tl, (j + 1) * tl)
    xo = x_ref[:, tr, :] + mod_ref[2] * y.reshape(nb, tl, D_MODEL)
    if final:
        ms = jnp.mean(xo * xo, axis=-1, keepdims=True)
        xo = xo * lax.rsqrt(ms + EPS) * fg_ref[...]
    xo_ref[:, tr, :] = xo


def _odd_kernel(x_ref, mod_ref, pst_ref, ng_ref, win_ref, gw_ref, gb_ref, sc_ref, wout_ref, fg_ref,
                xo_ref, po_ref,
                proj_scr, vbuf, pbuf, phist_scr, p_scr, z_scr, *, nb, tl, tps, pos0, final):
    step = pl.program_id(1)
    last = pl.num_programs(1) - 1

    @pl.when(step == 0)
    def _init():
        phist_scr[...] = jnp.zeros_like(phist_scr)
        phist_scr[:, POOL_HIST - POOL_PAD:POOL_HIST, :] = pst_ref[...]

    mixer = functools.partial(
        _odd_mixer, x_ref=x_ref, mod_ref=mod_ref, gw_ref=gw_ref, gb_ref=gb_ref, sc_ref=sc_ref,
        wout_ref=wout_ref, fg_ref=fg_ref, xo_ref=xo_ref, vbuf=vbuf, pbuf=pbuf,
        phist_scr=phist_scr, p_scr=p_scr, z_scr=z_scr, nb=nb, tl=tl, tps=tps, pos0=pos0, final=final)
    _run_tiles(x_ref, mod_ref, ng_ref, win_ref, proj_scr, mixer, tl=tl, tps=tps)

    @pl.when(step == last)
    def _fin():
        po_ref[...] = phist_scr[:, POOL_HIST - POOL_PAD:POOL_HIST, :]


def _odd_call(x, mod, pool_st, ng, w_in, gw, gb, sc, w_out, fg, *, nb, tl, tps, pos0, final):
    b, l, d = x.shape
    m = nb * tl
    n_tiles = l // tl
    assert b % nb == 0 and l % (tl * tps) == 0 and tl % SEG == 0 and m % (SEG * SEGS) == 0
    kern = functools.partial(_odd_kernel, nb=nb, tl=tl, tps=tps, pos0=pos0, final=final)
    groups = m // (SEG * SEGS)
    return pl.pallas_call(
        kern,
        grid=(b // nb, n_tiles // tps),
        in_specs=[
            pl.BlockSpec((nb, tps * tl, d), lambda i, s: (i, s, 0)),
            pl.BlockSpec((3, nb, 1, d), lambda i, s: (0, i, 0, 0)),
            pl.BlockSpec((nb, POOL_PAD, D_C), lambda i, s: (i, 0, 0)),
            _const_spec((1, d)),
            _const_spec((d, 2 * D_C)),
            _const_spec((len(POOL_WINDOWS), POOL_GW, POOL_GW)),
            _const_spec((1, D_C)),
            _const_spec((1, D_C)),
            _const_spec((D_C, d)),
            _const_spec((1, d)),
        ],
        out_specs=[
            pl.BlockSpec((nb, tps * tl, d), lambda i, s: (i, s, 0)),
            pl.BlockSpec((nb, POOL_PAD, D_C), lambda i, s: (i, 0, 0)),
        ],
        out_shape=[
            jax.ShapeDtypeStruct((b, l, d), F32),
            jax.ShapeDtypeStruct((b, POOL_PAD, D_C), F32),
        ],
        scratch_shapes=[
            pltpu.VMEM((min(tps, 2), m, 2 * D_C), F32),
            pltpu.VMEM((groups, P_SLAB, SEGS * POOL_PITCH, LANES), F32),
            pltpu.VMEM((groups, P_SLAB, SEGS * OUT_PITCH, LANES), F32),
            pltpu.VMEM((nb, POOL_HIST, D_C), F32),
            pltpu.VMEM((m, D_C), BF16),
            pltpu.VMEM((m, D_C), BF16),
        ],
        compiler_params=pltpu.CompilerParams(
            dimension_semantics=("arbitrary", "arbitrary"), vmem_limit_bytes=VMEM_LIMIT),
        name="odd_layer",
    )(x, mod, pool_st, ng, w_in, gw, gb, sc, w_out, fg)


PROMPT_TILE = dict(nb=2, tl=256, tps=1)
SAMPLE_TILE = dict(nb=8, tl=32, tps=1)


def kernel(x_prompt, x_sample, c_prompt, c_sample, state_conv, state_gla, state_pool, norm_g, ada_w, ada_b, ev_w_in, ev_conv_w, ev_conv_b, ev_ln_g, ev_ln_b, ev_gate_w2, ev_gate_b, ev_head_g, ev_w_out, od_w_in, od_group_w, od_group_b, od_scale, od_w_out, final_g):
    b_p = x_prompt.shape[0]
    b_s = x_sample.shape[0]
    n_even = ev_w_in.shape[0]
    n_odd = od_w_in.shape[0]

    c_all = jnp.concatenate([c_prompt, c_sample], axis=0)
    bp = -(-c_all.shape[0] // SUBLANES) * SUBLANES
    c_all = jnp.pad(c_all, ((0, bp - c_all.shape[0]), (0, 0)))
    mods = _ada_call(c_all, ada_w, ada_b)

    ev_w_in_b = jnp.pad(ev_w_in, ((0, 0), (0, 0), (0, EV_IN_PAD - EV_IN))).astype(BF16)
    ev_gw2_b = jnp.pad(ev_gate_w2, ((0, 0), (0, LANES - GATE_RANK), (0, 0))).astype(BF16)
    ev_w_out_b = ev_w_out.astype(BF16)
    od_w_in_b = od_w_in.astype(BF16)
    od_gw_b = od_group_w.astype(BF16)
    od_w_out_b = od_w_out.astype(BF16)

    zero_conv = jnp.zeros((b_p, CONV_PAD, D_A), F32)
    zero_gla = jnp.zeros((b_p, H_B, DK_B, DV_B), F32)
    zero_pool = jnp.zeros((b_p, POOL_PAD, D_C), F32)
    fg = final_g.reshape(1, D_MODEL)

    xp, xs = x_prompt, x_sample
    conv_p, gla_p, pool_p, conv_s, gla_s, pool_s = [], [], [], [], [], []
    for l in range(DEPTH):
        mod_p = mods[l, :, :b_p].reshape(3, b_p, 1, D_MODEL)
        mod_s = mods[l, :, b_p:b_p + b_s].reshape(3, b_s, 1, D_MODEL)
        ng = norm_g[l].reshape(1, D_MODEL)
        if l % 2 == 0:
            e = l // 2
            w = (ng, ev_w_in_b[e], ev_conv_w[e], ev_conv_b[e].reshape(1, D_A),
                 ev_ln_g[e].reshape(1, D_A), ev_ln_b[e].reshape(1, D_A), ev_gw2_b[e],
                 ev_gate_b[e].reshape(1, D_BK), ev_head_g[e].reshape(1, D_BV), ev_w_out_b[e])
            xp, cp, gp = _even_call(xp, mod_p, zero_conv, zero_gla, *w, **PROMPT_TILE)
            xs, cs, gs = _even_call(xs, mod_s, state_conv[e], state_gla[e], *w, **SAMPLE_TILE)
            conv_p.append(cp)
            gla_p.append(gp)
            conv_s.append(cs)
            gla_s.append(gs)
        else:
            o = l // 2
            final = l == DEPTH - 1
            w = (ng, od_w_in_b[o], od_gw_b[o], od_group_b[o].reshape(1, D_C),
                 od_scale[o].reshape(1, D_C), od_w_out_b[o], fg)
            xp, pp = _odd_call(xp, mod_p, zero_pool, *w, pos0=0, final=final, **PROMPT_TILE)
            xs, ps = _odd_call(xs, mod_s, state_pool[o], *w, pos0=PAST_LEN, final=final, **SAMPLE_TILE)
            pool_p.append(pp)
            pool_s.append(ps)
    assert n_even == len(conv_p) and n_odd == len(pool_p)
    return (xp, xs, jnp.stack(conv_p), jnp.stack(gla_p), jnp.stack(pool_p),
            jnp.stack(conv_s), jnp.stack(gla_s), jnp.stack(pool_s))
```

```python
import functools

import numpy as np
import jax
import jax.numpy as jnp
from jax import lax
from jax.experimental import pallas as pl
from jax.experimental.pallas import tpu as pltpu

F32 = jnp.float32
BF16 = jnp.bfloat16

LANES = 128
SUBLANES = 8
VMEM_LIMIT = 56 * 1024 * 1024
MXU_COLS = 256
GROUP_ROWS = MXU_COLS

D_MODEL = 1024
DEPTH = 4
EPS = 1e-6
PAST_LEN = 2048
LOG2E = 1.4426950408889634

D_A = D_MODEL // 2
CONV_W = 31
CONV_PAD = CONV_W - 1
SEG = 32
SEGS = SUBLANES
SEG_PITCH = 68
OUT_PITCH = 36
N_SLAB = D_A // LANES
CONV_KB = 4

H_B = 4
DK_B = D_MODEL // 16
DV_B = D_MODEL // 8
D_BK = H_B * DK_B
D_BV = H_B * DV_B
GATE_RANK = 16
GATE_TAU = 16.0
GLA_CHUNK = 64
SUB_ROWS = LANES

D_C = D_MODEL
POOL_WINDOWS = (2, 4, 8, 16)
POOL_GW = D_C // len(POOL_WINDOWS)
POOL_PAD = 15
POOL_HIST = 16
POOL_PITCH = 52
P_SLAB = D_C // LANES

O_VAL, O_GLU, O_GATE = 0, D_A, 2 * D_A
O_Q = 3 * D_A
O_K = O_Q + D_BK
O_V = O_K + D_BK
O_BG = O_V + D_BV
O_LR = O_BG + D_BV
EV_IN = O_LR + GATE_RANK
EV_IN_PAD = O_LR + LANES


def _sigmoid(x):
    return 1.0 / (1.0 + jnp.exp2(x * (-LOG2E)))


def _silu(x):
    return x * _sigmoid(x)


def _split_bf16(x):
    hi = x.astype(BF16)
    lo = (x - hi.astype(F32)).astype(BF16)
    return hi, lo


def _dot(a, b):
    return jnp.dot(a, b, preferred_element_type=F32)


def _const_spec(shape):
    nd = len(shape)
    return pl.BlockSpec(shape, lambda *_: (0,) * nd, pipeline_mode=pl.Buffered(1))


def _ada_kernel(c_ref, w_ref, b_ref, o_ref):
    c = c_ref[...]
    cs = _silu(c)
    cs_hi, cs_lo = _split_bf16(cs)
    w_hi, w_lo = _split_bf16(w_ref[0])
    acc = _dot(cs_hi, w_hi) + (_dot(cs_lo, w_hi) + _dot(cs_hi, w_lo))
    o_ref[0, 0] = acc + b_ref[0, 0]


def _ada_call(c_all, ada_w, ada_b):
    bp = c_all.shape[0]
    return pl.pallas_call(
        _ada_kernel,
        grid=(DEPTH, 3),
        in_specs=[
            pl.BlockSpec((bp, D_MODEL), lambda l, j: (0, 0)),
            pl.BlockSpec((1, D_MODEL, D_MODEL), lambda l, j: (l, 0, j)),
            pl.BlockSpec((1, 1, 1, D_MODEL), lambda l, j: (l, j, 0, 0)),
        ],
        out_specs=pl.BlockSpec((1, 1, bp, D_MODEL), lambda l, j: (l, j, 0, 0)),
        out_shape=jax.ShapeDtypeStruct((DEPTH, 3, bp, D_MODEL), F32),
        compiler_params=pltpu.CompilerParams(
            dimension_semantics=("arbitrary", "arbitrary"), vmem_limit_bytes=VMEM_LIMIT),
        name="ada_mod",
    )(c_all, ada_w, ada_b.reshape(DEPTH, 3, 1, D_MODEL))


def _modulated_input(x3, mod_ref, ng_ref):
    nb, tl, d = x3.shape
    shift = mod_ref[0]
    scale = mod_ref[1]
    ms = jnp.mean(x3 * x3, axis=-1, keepdims=True)
    gs = ng_ref[...] * (1.0 + scale)
    h = (x3 * lax.rsqrt(ms + EPS)) * gs + shift
    return h.astype(BF16).reshape(nb * tl, d)


def _run_tiles(x_ref, mod_ref, ng_ref, win_ref, proj_scr, mixer, *, tl, tps):
    for j in range(tps):
        hb = _modulated_input(x_ref[:, j * tl:(j + 1) * tl, :], mod_ref, ng_ref)
        proj_scr[j % 2] = _dot(hb, win_ref[...])
        mixer(proj_scr.at[j % 2], j)


def _even_mixer(proj, j, x_ref, mod_ref, cw_ref, cb_ref, lng_ref, lnb_ref, gw2_ref, gb_ref,
                hg_ref, wout_ref, tri_ref, ones_ref, xo_ref,
                cbuf, obuf, hist_scr, st_scr, o_scr, y_scr, *, nb, tl, chunk):
    m = nb * tl
    gr = min(m, GROUP_ROWS)
    lane = lax.broadcasted_iota(jnp.int32, (1, LANES), 1)
    lo_half = lane < DK_B

    segs_per_seq = tl // SEG
    prev = None
    for gi in range(m // SEG):
        g, i = divmod(gi, SEGS)
        n, si = divmod(gi, segs_per_seq)
        rows = slice(gi * SEG, (gi + 1) * SEG)
        blk = proj[rows, O_VAL:O_VAL + D_A] * _sigmoid(proj[rows, O_GLU:O_GLU + D_A])
        halo = hist_scr[n] if si == 0 else prev
        for s in range(N_SLAB):
            ls = slice(s * LANES, (s + 1) * LANES)
            cbuf[g, s, i * SEG_PITCH:i * SEG_PITCH + SEG, :] = halo[:, ls]
            cbuf[g, s, i * SEG_PITCH + SEG:i * SEG_PITCH + 2 * SEG, :] = blk[:, ls]
        if si == segs_per_seq - 1:
            hist_scr[n] = blk
        prev = blk

    inv_da = 1.0 / D_A
    for g in range(m // (SEG * SEGS)):
        for kb in range(SEG // CONV_KB):
            acc = [[jnp.broadcast_to(cb_ref[:, s * LANES:(s + 1) * LANES], (SEGS, LANES))
                    for s in range(N_SLAB)] for _ in range(CONV_KB)]
            for jt in range(CONV_W):
                for s in range(N_SLAB):
                    wj = cw_ref[jt:jt + 1, s * LANES:(s + 1) * LANES]
                    for kk in range(CONV_KB):
                        start = SEG - CONV_PAD + kb * CONV_KB + kk + jt
                        acc[kk][s] = acc[kk][s] + wj * cbuf[g, s, pl.ds(start, SEGS, stride=SEG_PITCH), :]
            for kk in range(CONV_KB):
                a = acc[kk]
                mu = jnp.sum(a[0] + a[1] + a[2] + a[3], axis=-1, keepdims=True) * inv_da
                xc = [v - mu for v in a]
                sq = xc[0] * xc[0] + xc[1] * xc[1] + xc[2] * xc[2] + xc[3] * xc[3]
                rstd = lax.rsqrt(jnp.sum(sq, axis=-1, keepdims=True) * inv_da + EPS)
                for s in range(N_SLAB):
                    ls = slice(s * LANES, (s + 1) * LANES)
                    yn = xc[s] * rstd * lng_ref[:, ls] + lnb_ref[:, ls]
                    obuf[g, s, pl.ds(kb * CONV_KB + kk, SEGS, stride=OUT_PITCH), :] = _silu(yn)

    for gi in range(m // SEG):
        g, i = divmod(gi, SEGS)
        rows = slice(gi * SEG, (gi + 1) * SEG)
        for s in range(N_SLAB):
            ya = obuf[g, s, i * OUT_PITCH:i * OUT_PITCH + SEG, :] * _silu(
                proj[rows, O_GATE + s * LANES:O_GATE + (s + 1) * LANES])
            y_scr[rows, s * LANES:(s + 1) * LANES] = ya.astype(BF16)

    pre = _dot(proj[:, O_LR:O_LR + LANES].astype(BF16), gw2_ref[...]) + gb_ref[...]
    la = (jnp.minimum(pre, 0.0) - jnp.log1p(jnp.exp(-jnp.abs(pre)))) * (1.0 / GATE_TAU)
    la_hi, la_lo = _split_bf16(la)
    tri = tri_ref[...]
    ones_bd = ones_ref[...]
    causal = tri_ref[0:SUB_ROWS, 0:SUB_ROWS].astype(F32) > 0.5
    pair_lo = jnp.concatenate([lo_half, lo_half], axis=1)
    srow = lax.broadcasted_iota(jnp.int32, (LANES, 2 * DV_B), 0)
    scol = lax.broadcasted_iota(jnp.int32, (LANES, 2 * DV_B), 1)
    on_diag = (srow < DK_B) == (scol < DV_B)
    in_chunk = [(lane >= c * chunk) & (lane < (c + 1) * chunk) for c in range(SUB_ROWS // chunk)]
    for g in range(m // gr):
        rows = slice(g * gr, (g + 1) * gr)
        b = _dot(tri, la_hi[rows]) + _dot(tri, la_lo[rows])
        bl = _dot(ones_bd, la_hi[rows]) + _dot(ones_bd, la_lo[rows])
        k = proj[rows, O_K:O_K + D_BK]
        q_e = proj[rows, O_Q:O_Q + D_BK] * (DK_B ** -0.5) * jnp.exp(b)
        q_pair = q_e.astype(BF16)
        q_even = jnp.where(pair_lo, q_e, 0.0).astype(BF16)
        q_odd = jnp.where(pair_lo, 0.0, q_e).astype(BF16)
        k_e = k * jnp.exp(-b)
        k_l = k * jnp.exp(bl - b)
        dec = jnp.exp(bl)
        vb = proj[rows, O_V:O_V + D_BV].astype(BF16)
        for sg in range(gr // SUB_ROWS):
            sr = slice(sg * SUB_ROWS, (sg + 1) * SUB_ROWS)
            for p in range(H_B // 2):
                pc = slice(p * LANES, (p + 1) * LANES)
                vc = slice(p * 2 * DV_B, (p + 1) * 2 * DV_B)
                k_et = k_e[sr, pc].T.astype(BF16)
                k_lt = k_l[sr, pc].T
                v2 = vb[sr, vc]
                att = _dot(jnp.concatenate([q_even[sr, pc], q_odd[sr, pc]], axis=0), k_et)
                att_a = jnp.where(causal, att[0:SUB_ROWS], 0.0).astype(BF16)
                att_b = jnp.where(causal, att[SUB_ROWS:2 * SUB_ROWS], 0.0).astype(BF16)
                o_intra = jnp.concatenate(
                    [_dot(att_a, v2[:, 0:DV_B]), _dot(att_b, v2[:, DV_B:2 * DV_B])], axis=1)
                for c in range(SUB_ROWS // chunk):
                    c0 = sg * SUB_ROWS + c * chunk
                    row0 = g * gr + c0
                    n = row0 // tl
                    s_bd = st_scr[n, p]
                    o_inter = _dot(q_pair[c0:c0 + chunk, pc], s_bd.astype(BF16))
                    o_scr[row0:row0 + chunk, vc] = o_intra[c * chunk:(c + 1) * chunk] + o_inter
                    upd = _dot(jnp.where(in_chunk[c], k_lt, 0.0).astype(BF16), v2)
                    dcol = jnp.broadcast_to(dec[c0:c0 + 1, pc], (LANES, LANES)).T
                    st_scr[n, p] = (jnp.concatenate([dcol, dcol], axis=1) * s_bd
                                    + jnp.where(on_diag, upd, 0.0))

    for hh in range(H_B):
        vc = slice(hh * DV_B, (hh + 1) * DV_B)
        oh = o_scr[:, vc]
        ms = jnp.mean(oh * oh, axis=-1, keepdims=True)
        on = oh * lax.rsqrt(ms + EPS) * hg_ref[:, vc]
        yb = on * _silu(proj[:, O_BG + hh * DV_B:O_BG + (hh + 1) * DV_B])
        y_scr[:, D_A + hh * DV_B:D_A + (hh + 1) * DV_B] = yb.astype(BF16)

    y = _dot(y_scr[...], wout_ref[...])
    tr = slice(j * tl, (j + 1) * tl)
    xo_ref[:, tr, :] = x_ref[:, tr, :] + mod_ref[2] * y.reshape(nb, tl, D_MODEL)


def _even_kernel(x_ref, mod_ref, cst_ref, gst_ref, ng_ref, win_ref, cw_ref, cb_ref, lng_ref,
                 lnb_ref, gw2_ref, gb_ref, hg_ref, wout_ref, tri_ref, ones_ref,
                 xo_ref, co_ref, go_ref,
                 proj_scr, cbuf, obuf, hist_scr, st_scr, o_scr, y_scr, *, nb, tl, tps, chunk):
    step = pl.program_id(1)
    last = pl.num_programs(1) - 1

    @pl.when(step == 0)
    def _init():
        hist_scr[...] = jnp.zeros_like(hist_scr)
        hist_scr[:, SEG - CONV_PAD:SEG, :] = cst_ref[...]
        z = jnp.zeros((DK_B, DV_B), F32)
        for n in range(nb):
            for p in range(H_B // 2):
                top = jnp.concatenate([gst_ref[n, 2 * p], z], axis=1)
                bot = jnp.concatenate([z, gst_ref[n, 2 * p + 1]], axis=1)
                st_scr[n, p] = jnp.concatenate([top, bot], axis=0)

    mixer = functools.partial(
        _even_mixer, x_ref=x_ref, mod_ref=mod_ref, cw_ref=cw_ref, cb_ref=cb_ref, lng_ref=lng_ref,
        lnb_ref=lnb_ref, gw2_ref=gw2_ref, gb_ref=gb_ref, hg_ref=hg_ref, wout_ref=wout_ref,
        tri_ref=tri_ref, ones_ref=ones_ref, xo_ref=xo_ref, cbuf=cbuf, obuf=obuf,
        hist_scr=hist_scr, st_scr=st_scr, o_scr=o_scr, y_scr=y_scr, nb=nb, tl=tl, chunk=chunk)
    _run_tiles(x_ref, mod_ref, ng_ref, win_ref, proj_scr, mixer, tl=tl, tps=tps)

    @pl.when(step == last)
    def _fin():
        co_ref[...] = hist_scr[:, SEG - CONV_PAD:SEG, :]
        for n in range(nb):
            for p in range(H_B // 2):
                go_ref[n, 2 * p] = st_scr[n, p, 0:DK_B, 0:DV_B]
                go_ref[n, 2 * p + 1] = st_scr[n, p, DK_B:2 * DK_B, DV_B:2 * DV_B]


def _decay_matrices(rows, chunk):
    i = np.arange(rows)
    same = (i[:, None] // chunk) == (i[None, :] // chunk)
    tri = same & (i[:, None] >= i[None, :])
    return jnp.asarray(tri, BF16), jnp.asarray(same, BF16)


def _even_call(x, mod, conv_st, gla_st, ng, w_in, cw, cb, lng, lnb, gw2, gb, hg, w_out, *, nb, tl, tps):
    b, l, d = x.shape
    chunk = min(GLA_CHUNK, l)
    m = nb * tl
    gr = min(m, GROUP_ROWS)
    n_tiles = l // tl
    assert b % nb == 0 and l % (tl * tps) == 0 and tl % chunk == 0 and m % gr == 0 and gr % chunk == 0
    assert tl % SEG == 0 and m % (SEG * SEGS) == 0
    assert gr % SUB_ROWS == 0 and SUB_ROWS % chunk == 0 and 2 * DK_B == LANES
    tri, ones_bd = _decay_matrices(gr, chunk)
    kern = functools.partial(_even_kernel, nb=nb, tl=tl, tps=tps, chunk=chunk)
    groups = m // (SEG * SEGS)
    return pl.pallas_call(
        kern,
        grid=(b // nb, n_tiles // tps),
        in_specs=[
            pl.BlockSpec((nb, tps * tl, d), lambda i, s: (i, s, 0)),
            pl.BlockSpec((3, nb, 1, d), lambda i, s: (0, i, 0, 0)),
            pl.BlockSpec((nb, CONV_PAD, D_A), lambda i, s: (i, 0, 0)),
            pl.BlockSpec((nb, H_B, DK_B, DV_B), lambda i, s: (i, 0, 0, 0)),
            _const_spec((1, d)),
            _const_spec((d, EV_IN_PAD)),
            _const_spec((CONV_W, D_A)),
            _const_spec((1, D_A)),
            _const_spec((1, D_A)),
            _const_spec((1, D_A)),
            _const_spec((LANES, D_BK)),
            _const_spec((1, D_BK)),
            _const_spec((1, D_BV)),
            _const_spec((D_A + D_BV, d)),
            _const_spec((gr, gr)),
            _const_spec((gr, gr)),
        ],
        out_specs=[
            pl.BlockSpec((nb, tps * tl, d), lambda i, s: (i, s, 0)),
            pl.BlockSpec((nb, CONV_PAD, D_A), lambda i, s: (i, 0, 0)),
            pl.BlockSpec((nb, H_B, DK_B, DV_B), lambda i, s: (i, 0, 0, 0)),
        ],
        out_shape=[
            jax.ShapeDtypeStruct((b, l, d), F32),
            jax.ShapeDtypeStruct((b, CONV_PAD, D_A), F32),
            jax.ShapeDtypeStruct((b, H_B, DK_B, DV_B), F32),
        ],
        scratch_shapes=[
            pltpu.VMEM((min(tps, 2), m, EV_IN_PAD), F32),
            pltpu.VMEM((groups, N_SLAB, SEGS * SEG_PITCH, LANES), F32),
            pltpu.VMEM((groups, N_SLAB, SEGS * OUT_PITCH, LANES), F32),
            pltpu.VMEM((nb, SEG, D_A), F32),
            pltpu.VMEM((nb, H_B // 2, LANES, 2 * DV_B), F32),
            pltpu.VMEM((m, D_BV), F32),
            pltpu.VMEM((m, D_A + D_BV), BF16),
        ],
        compiler_params=pltpu.CompilerParams(
            dimension_semantics=("arbitrary", "arbitrary"), vmem_limit_bytes=VMEM_LIMIT),
        name="even_layer",
    )(x, mod, conv_st, gla_st, ng, w_in, cw, cb, lng, lnb, gw2, gb, hg, w_out, tri, ones_bd)


def _odd_mixer(proj, j, x_ref, mod_ref, gw_ref, gb_ref, sc_ref, wout_ref, fg_ref, xo_ref,
               vbuf, pbuf, phist_scr, p_scr, z_scr, *, nb, tl, tps, pos0, final):
    m = nb * tl
    tile_pos = pos0 + (pl.program_id(1) * tps + j) * tl

    segs_per_seq = tl // SEG
    prev = None
    for gi in range(m // SEG):
        g, i = divmod(gi, SEGS)
        n, si = divmod(gi, segs_per_seq)
        blk = proj[gi * SEG:(gi + 1) * SEG, 0:D_C]
        halo = phist_scr[n] if si == 0 else prev[SEG - POOL_HIST:SEG, :]
        for s in range(P_SLAB):
            ls = slice(s * LANES, (s + 1) * LANES)
            vbuf[g, s, i * POOL_PITCH:i * POOL_PITCH + POOL_HIST, :] = halo[:, ls]
            vbuf[g, s, i * POOL_PITCH + POOL_HIST:i * POOL_PITCH + POOL_HIST + SEG, :] = blk[:, ls]
        if si == segs_per_seq - 1:
            phist_scr[n] = blk[SEG - POOL_HIST:SEG, :]
        prev = blk

    sub = lax.broadcasted_iota(jnp.int32, (SEGS, LANES), 0)
    for g in range(m // (SEG * SEGS)):
        seg_pos1 = tile_pos + 1 + ((sub + g * SEGS) % segs_per_seq) * SEG
        for wi, w in enumerate(POOL_WINDOWS):
            for s in range(wi * (POOL_GW // LANES), (wi + 1) * (POOL_GW // LANES)):
                lo = -(w - 1)
                cur = {k: vbuf[g, s, pl.ds(POOL_HIST + k, SEGS, stride=POOL_PITCH), :]
                       for k in range(lo, SEG)}
                xk = cur
                d = 1
                while d < w:
                    lo += d
                    cur = {k: cur[k] + cur[k - d] for k in range(lo, SEG)}
                    d *= 2
                for k in range(SEG):
                    if k >= w - 1:
                        p = cur[k] * (1.0 / w) - xk[k]
                    else:
                        cnt = jnp.minimum(seg_pos1 + k, w).astype(F32)
                        p = cur[k] / cnt - xk[k]
                    pbuf[g, s, pl.ds(k, SEGS, stride=OUT_PITCH), :] = p

    for gi in range(m // SEG):
        g, i = divmod(gi, SEGS)
        rows = slice(gi * SEG, (gi + 1) * SEG)
        for s in range(P_SLAB):
            p_scr[rows, s * LANES:(s + 1) * LANES] = pbuf[
                g, s, i * OUT_PITCH:i * OUT_PITCH + SEG, :].astype(BF16)

    for gi in range(len(POOL_WINDOWS)):
        cs = slice(gi * POOL_GW, (gi + 1) * POOL_GW)
        pg = _dot(p_scr[:, cs], gw_ref[gi]) + gb_ref[:, cs]
        z = pg * sc_ref[:, cs] * _silu(proj[:, D_C + gi * POOL_GW:D_C + (gi + 1) * POOL_GW])
        z_scr[:, cs] = z.astype(BF16)

    y = _dot(z_scr[...], wout_ref[...])
    tr = slice(j * tl, (j + 1) * tl)
    xo = x_ref[:, tr, :] + mod_ref[2] * y.reshape(nb, tl, D_MODEL)
    if final:
        ms = jnp.mean(xo * xo, axis=-1, keepdims=True)
        xo = xo * lax.rsqrt(ms + EPS) * fg_ref[...]
    xo_ref[:, tr, :] = xo


def _odd_kernel(x_ref, mod_ref, pst_ref, ng_ref, win_ref, gw_ref, gb_ref, sc_ref, wout_ref, fg_ref,
                xo_ref, po_ref,
                proj_scr, vbuf, pbuf, phist_scr, p_scr, z_scr, *, nb, tl, tps, pos0, final):
    step = pl.program_id(1)
    last = pl.num_programs(1) - 1

    @pl.when(step == 0)
    def _init():
        phist_scr[...] = jnp.zeros_like(phist_scr)
        phist_scr[:, POOL_HIST - POOL_PAD:POOL_HIST, :] = pst_ref[...]

    mixer = functools.partial(
        _odd_mixer, x_ref=x_ref, mod_ref=mod_ref, gw_ref=gw_ref, gb_ref=gb_ref, sc_ref=sc_ref,
        wout_ref=wout_ref, fg_ref=fg_ref, xo_ref=xo_ref, vbuf=vbuf, pbuf=pbuf,
        phist_scr=phist_scr, p_scr=p_scr, z_scr=z_scr, nb=nb, tl=tl, tps=tps, pos0=pos0, final=final)
    _run_tiles(x_ref, mod_ref, ng_ref, win_ref, proj_scr, mixer, tl=tl, tps=tps)

    @pl.when(step == last)
    def _fin():
        po_ref[...] = phist_scr[:, POOL_HIST - POOL_PAD:POOL_HIST, :]


def _odd_call(x, mod, pool_st, ng, w_in, gw, gb, sc, w_out, fg, *, nb, tl, tps, pos0, final):
    b, l, d = x.shape
    m = nb * tl
    n_tiles = l // tl
    assert b % nb == 0 and l % (tl * tps) == 0 and tl % SEG == 0 and m % (SEG * SEGS) == 0
    kern = functools.partial(_odd_kernel, nb=nb, tl=tl, tps=tps, pos0=pos0, final=final)
    groups = m // (SEG * SEGS)
    return pl.pallas_call(
        kern,
        grid=(b // nb, n_tiles // tps),
        in_specs=[
            pl.BlockSpec((nb, tps * tl, d), lambda i, s: (i, s, 0)),
            pl.BlockSpec((3, nb, 1, d), lambda i, s: (0, i, 0, 0)),
            pl.BlockSpec((nb, POOL_PAD, D_C), lambda i, s: (i, 0, 0)),
            _const_spec((1, d)),
            _const_spec((d, 2 * D_C)),
            _const_spec((len(POOL_WINDOWS), POOL_GW, POOL_GW)),
            _const_spec((1, D_C)),
            _const_spec((1, D_C)),
            _const_spec((D_C, d)),
            _const_spec((1, d)),
        ],
        out_specs=[
            pl.BlockSpec((nb, tps * tl, d), lambda i, s: (i, s, 0)),
            pl.BlockSpec((nb, POOL_PAD, D_C), lambda i, s: (i, 0, 0)),
        ],
        out_shape=[
            jax.ShapeDtypeStruct((b, l, d), F32),
            jax.ShapeDtypeStruct((b, POOL_PAD, D_C), F32),
        ],
        scratch_shapes=[
            pltpu.VMEM((min(tps, 2), m, 2 * D_C), F32),
            pltpu.VMEM((groups, P_SLAB, SEGS * POOL_PITCH, LANES), F32),
            pltpu.VMEM((groups, P_SLAB, SEGS * OUT_PITCH, LANES), F32),
            pltpu.VMEM((nb, POOL_HIST, D_C), F32),
            pltpu.VMEM((m, D_C), BF16),
            pltpu.VMEM((m, D_C), BF16),
        ],
        compiler_params=pltpu.CompilerParams(
            dimension_semantics=("arbitrary", "arbitrary"), vmem_limit_bytes=VMEM_LIMIT),
        name="odd_layer",
    )(x, mod, pool_st, ng, w_in, gw, gb, sc, w_out, fg)


PROMPT_TILE = dict(nb=2, tl=256, tps=1)
SAMPLE_TILE = dict(nb=8, tl=32, tps=1)


def kernel(x_prompt, x_sample, c_prompt, c_sample, state_conv, state_gla, state_pool, norm_g, ada_w, ada_b, ev_w_in, ev_conv_w, ev_conv_b, ev_ln_g, ev_ln_b, ev_gate_w2, ev_gate_b, ev_head_g, ev_w_out, od_w_in, od_group_w, od_group_b, od_scale, od_w_out, final_g):
    b_p = x_prompt.shape[0]
    b_s = x_sample.shape[0]
    n_even = ev_w_in.shape[0]
    n_odd = od_w_in.shape[0]

    c_all = jnp.concatenate([c_prompt, c_sample], axis=0)
    bp = -(-c_all.shape[0] // SUBLANES) * SUBLANES
    c_all = jnp.pad(c_all, ((0, bp - c_all.shape[0]), (0, 0)))
    mods = _ada_call(c_all, ada_w, ada_b)

    ev_w_in_b = jnp.pad(ev_w_in, ((0, 0), (0, 0), (0, EV_IN_PAD - EV_IN))).astype(BF16)
    ev_gw2_b = jnp.pad(ev_gate_w2, ((0, 0), (0, LANES - GATE_RANK), (0, 0))).astype(BF16)
    ev_w_out_b = ev_w_out.astype(BF16)
    od_w_in_b = od_w_in.astype(BF16)
    od_gw_b = od_group_w.astype(BF16)
    od_w_out_b = od_w_out.astype(BF16)

    zero_conv = jnp.zeros((b_p, CONV_PAD, D_A), F32)
    zero_gla = jnp.zeros((b_p, H_B, DK_B, DV_B), F32)
    zero_pool = jnp.zeros((b_p, POOL_PAD, D_C), F32)
    fg = final_g.reshape(1, D_MODEL)

    xp, xs = x_prompt, x_sample
    conv_p, gla_p, pool_p, conv_s, gla_s, pool_s = [], [], [], [], [], []
    for l in range(DEPTH):
        mod_p = mods[l, :, :b_p].reshape(3, b_p, 1, D_MODEL)
        mod_s = mods[l, :, b_p:b_p + b_s].reshape(3, b_s, 1, D_MODEL)
        ng = norm_g[l].reshape(1, D_MODEL)
        if l % 2 == 0:
            e = l // 2
            w = (ng, ev_w_in_b[e], ev_conv_w[e], ev_conv_b[e].reshape(1, D_A),
                 ev_ln_g[e].reshape(1, D_A), ev_ln_b[e].reshape(1, D_A), ev_gw2_b[e],
                 ev_gate_b[e].reshape(1, D_BK), ev_head_g[e].reshape(1, D_BV), ev_w_out_b[e])
            xp, cp, gp = _even_call(xp, mod_p, zero_conv, zero_gla, *w, **PROMPT_TILE)
            xs, cs, gs = _even_call(xs, mod_s, state_conv[e], state_gla[e], *w, **SAMPLE_TILE)
            conv_p.append(cp)
            gla_p.append(gp)
            conv_s.append(cs)
            gla_s.append(gs)
        else:
            o = l // 2
            final = l == DEPTH - 1
            w = (ng, od_w_in_b[o], od_gw_b[o], od_group_b[o].reshape(1, D_C),
                 od_scale[o].reshape(1, D_C), od_w_out_b[o], fg)
            xp, pp = _odd_call(xp, mod_p, zero_pool, *w, pos0=0, final=final, **PROMPT_TILE)
            xs, ps = _odd_call(xs, mod_s, state_pool[o], *w, pos0=PAST_LEN, final=final, **SAMPLE_TILE)
            pool_p.append(pp)
            pool_s.append(ps)
    assert n_even == len(conv_p) and n_odd == len(pool_p)
    return (xp, xs, jnp.stack(conv_p), jnp.stack(gla_p), jnp.stack(pool_p),
            jnp.stack(conv_s), jnp.stack(gla_s), jnp.stack(pool_s))
```

```python
import functools

import numpy as np
import jax
import jax.numpy as jnp
from jax import lax
from jax.experimental import pallas as pl
from jax.experimental.pallas import tpu as pltpu

F32 = jnp.float32
BF16 = jnp.bfloat16

LANES = 128
SUBLANES = 8
VMEM_LIMIT = 56 * 1024 * 1024
MXU_COLS = 256
GROUP_ROWS = MXU_COLS

D_MODEL = 1024
DEPTH = 4
EPS = 1e-6
PAST_LEN = 2048
LOG2E = 1.4426950408889634

D_A = D_MODEL // 2
CONV_W = 31
CONV_PAD = CONV_W - 1
SEG = 32
SEGS = SUBLANES
SEG_PITCH = 68
OUT_PITCH = 36
N_SLAB = D_A // LANES
CONV_KB = 4

H_B = 4
DK_B = D_MODEL // 16
DV_B = D_MODEL // 8
D_BK = H_B * DK_B
D_BV = H_B * DV_B
GATE_RANK = 16
GATE_TAU = 16.0
GLA_CHUNK = 64
SUB_ROWS = LANES

D_C = D_MODEL
POOL_WINDOWS = (2, 4, 8, 16)
POOL_GW = D_C // len(POOL_WINDOWS)
POOL_PAD = 15
POOL_HIST = 16
POOL_PITCH = 52
P_SLAB = D_C // LANES

O_VAL, O_GLU, O_GATE = 0, D_A, 2 * D_A
O_Q = 3 * D_A
O_K = O_Q + D_BK
O_V = O_K + D_BK
O_BG = O_V + D_BV
O_LR = O_BG + D_BV
EV_IN = O_LR + GATE_RANK
EV_IN_PAD = O_LR + LANES


def _sigmoid(x):
    return 1.0 / (1.0 + jnp.exp2(x * (-LOG2E)))


def _silu(x):
    return x * _sigmoid(x)


def _split_bf16(x):
    hi = x.astype(BF16)
    lo = (x - hi.astype(F32)).astype(BF16)
    return hi, lo


def _dot(a, b):
    return jnp.dot(a, b, preferred_element_type=F32)


def _const_spec(shape):
    nd = len(shape)
    return pl.BlockSpec(shape, lambda *_: (0,) * nd, pipeline_mode=pl.Buffered(1))


def _layer_spec(shape, layer):
    nd = len(shape)
    return pl.BlockSpec((None,) + tuple(shape), lambda *_: (layer,) + (0,) * nd,
                        pipeline_mode=pl.Buffered(1))


def _ada_kernel(c_ref, w_ref, b_ref, o_ref):
    c = c_ref[...]
    cs = _silu(c)
    cs_hi, cs_lo = _split_bf16(cs)
    w_hi, w_lo = _split_bf16(w_ref[0])
    acc = _dot(cs_hi, w_hi) + (_dot(cs_lo, w_hi) + _dot(cs_hi, w_lo))
    o_ref[0, 0] = acc + b_ref[0, 0]


def _ada_call(c_all, ada_w, ada_b):
    bp = c_all.shape[0]
    return pl.pallas_call(
        _ada_kernel,
        grid=(DEPTH, 3),
        in_specs=[
            pl.BlockSpec((bp, D_MODEL), lambda l, j: (0, 0)),
            pl.BlockSpec((1, D_MODEL, D_MODEL), lambda l, j: (l, 0, j)),
            pl.BlockSpec((1, 1, 1, D_MODEL), lambda l, j: (l, j, 0, 0)),
        ],
        out_specs=pl.BlockSpec((1, 1, bp, D_MODEL), lambda l, j: (l, j, 0, 0)),
        out_shape=jax.ShapeDtypeStruct((DEPTH, 3, bp, D_MODEL), F32),
        compiler_params=pltpu.CompilerParams(
            dimension_semantics=("arbitrary", "arbitrary"), vmem_limit_bytes=VMEM_LIMIT),
        name="ada_mod",
    )(c_all, ada_w, ada_b.reshape(DEPTH, 3, 1, D_MODEL))


def _modulated_input(x3, mod_ref, ng_ref):
    nb, tl, d = x3.shape
    shift = mod_ref[0]
    scale = mod_ref[1]
    ms = jnp.mean(x3 * x3, axis=-1, keepdims=True)
    gs = ng_ref[...] * (1.0 + scale)
    h = (x3 * lax.rsqrt(ms + EPS)) * gs + shift
    return h.astype(BF16).reshape(nb * tl, d)


def _cols(proj, rows, c0, c1):
    parts = []
    while c0 < c1:
        p, off = divmod(c0, MXU_COLS)
        w = min(MXU_COLS - off, c1 - c0)
        parts.append(proj[p, rows, off:off + w])
        c0 += w
    return parts[0] if len(parts) == 1 else jnp.concatenate(parts, axis=1)


def _run_tiles(x_ref, mod_ref, ng_ref, win_ref, proj_scr, group_stage, rest_stage, *, tl, tps):
    width = win_ref.shape[1]
    for j in range(tps):
        hb = _modulated_input(x_ref[:, j * tl:(j + 1) * tl, :], mod_ref, ng_ref)
        proj = proj_scr.at[j % 2]
        groups = hb.shape[0] // GROUP_ROWS
        for g in range(groups):
            rows = slice(g * GROUP_ROWS, (g + 1) * GROUP_ROWS)
            for p, c0 in enumerate(range(0, width, MXU_COLS)):
                c1 = min(c0 + MXU_COLS, width)
                proj[p, rows, 0:c1 - c0] = _dot(hb[rows], win_ref[:, c0:c1])
        for g in range(groups):
            group_stage(proj, j, g)
        rest_stage(proj, j)


def _even_conv(proj, j, g, cw_ref, cb_ref, lng_ref, lnb_ref, cbuf, obuf, hist_scr, y_scr,
               *, tl):
    del j
    segs_per_seq = tl // SEG
    prev = None
    for i in range(SEGS):
        gi = g * SEGS + i
        n, si = divmod(gi, segs_per_seq)
        rows = slice(gi * SEG, (gi + 1) * SEG)
        blk = _cols(proj, rows, O_VAL, O_VAL + D_A) * _sigmoid(_cols(proj, rows, O_GLU, O_GLU + D_A))
        halo = hist_scr[n] if prev is None or si == 0 else prev
        for s in range(N_SLAB):
            ls = slice(s * LANES, (s + 1) * LANES)
            cbuf[g, s,i * SEG_PITCH:i * SEG_PITCH + SEG, :] = halo[:, ls]
            cbuf[g, s,i * SEG_PITCH + SEG:i * SEG_PITCH + 2 * SEG, :] = blk[:, ls]
        if i == SEGS - 1 or si == segs_per_seq - 1:
            hist_scr[n] = blk
        prev = blk

    inv_da = 1.0 / D_A
    for kb in range(SEG // CONV_KB):
        acc = [[jnp.broadcast_to(cb_ref[:, s * LANES:(s + 1) * LANES], (SEGS, LANES))
                for s in range(N_SLAB)] for _ in range(CONV_KB)]
        for jt in range(CONV_W):
            for s in range(N_SLAB):
                wj = cw_ref[jt:jt + 1, s * LANES:(s + 1) * LANES]
                for kk in range(CONV_KB):
                    start = SEG - CONV_PAD + kb * CONV_KB + kk + jt
                    acc[kk][s] = acc[kk][s] + wj * cbuf[g, s, pl.ds(start, SEGS, stride=SEG_PITCH), :]
        for kk in range(CONV_KB):
            a = acc[kk]
            mu = jnp.sum(a[0] + a[1] + a[2] + a[3], axis=-1, keepdims=True) * inv_da
            xc = [v - mu for v in a]
            sq = xc[0] * xc[0] + xc[1] * xc[1] + xc[2] * xc[2] + xc[3] * xc[3]
            rstd = lax.rsqrt(jnp.sum(sq, axis=-1, keepdims=True) * inv_da + EPS)
            for s in range(N_SLAB):
                ls = slice(s * LANES, (s + 1) * LANES)
                yn = xc[s] * rstd * lng_ref[:, ls] + lnb_ref[:, ls]
                obuf[g, s, pl.ds(kb * CONV_KB + kk, SEGS, stride=OUT_PITCH), :] = _silu(yn)

    for i in range(SEGS):
        gi = g * SEGS + i
        rows = slice(gi * SEG, (gi + 1) * SEG)
        for s in range(N_SLAB):
            ya = obuf[g, s,i * OUT_PITCH:i * OUT_PITCH + SEG, :] * _silu(
                _cols(proj, rows, O_GATE + s * LANES, O_GATE + (s + 1) * LANES))
            y_scr[rows, s * LANES:(s + 1) * LANES] = ya.astype(BF16)


def _even_rest(proj, j, x_ref, mod_ref, gw2_ref, gb_ref, hg_ref, wout_ref, tri_ref, ones_ref,
               xo_ref, st_scr, o_scr, y_scr, *, nb, tl, chunk):
    m = nb * tl
    gr = min(m, GROUP_ROWS)
    lane = lax.broadcasted_iota(jnp.int32, (1, LANES), 1)
    lo_half = lane < DK_B

    every = slice(None)
    pre = _dot(_cols(proj, every, O_LR, O_LR + LANES).astype(BF16), gw2_ref[...]) + gb_ref[...]
    la = (jnp.minimum(pre, 0.0) - jnp.log1p(jnp.exp(-jnp.abs(pre)))) * (1.0 / GATE_TAU)
    la_hi, la_lo = _split_bf16(la)
    tri = tri_ref[...]
    ones_bd = ones_ref[...]
    causal = tri_ref[0:SUB_ROWS, 0:SUB_ROWS].astype(F32) > 0.5
    pair_lo = jnp.concatenate([lo_half, lo_half], axis=1)
    srow = lax.broadcasted_iota(jnp.int32, (LANES, 2 * DV_B), 0)
    scol = lax.broadcasted_iota(jnp.int32, (LANES, 2 * DV_B), 1)
    on_diag = (srow < DK_B) == (scol < DV_B)
    in_chunk = [(lane >= c * chunk) & (lane < (c + 1) * chunk) for c in range(SUB_ROWS // chunk)]
    for g in range(m // gr):
        rows = slice(g * gr, (g + 1) * gr)
        b = _dot(tri, la_hi[rows]) + _dot(tri, la_lo[rows])
        bl = _dot(ones_bd, la_hi[rows]) + _dot(ones_bd, la_lo[rows])
        k = _cols(proj, rows, O_K, O_K + D_BK)
        q_e = _cols(proj, rows, O_Q, O_Q + D_BK) * (DK_B ** -0.5) * jnp.exp(b)
        q_pair = q_e.astype(BF16)
        q_even = jnp.where(pair_lo, q_e, 0.0).astype(BF16)
        q_odd = jnp.where(pair_lo, 0.0, q_e).astype(BF16)
        k_e = k * jnp.exp(-b)
        k_l = k * jnp.exp(bl - b)
        dec = jnp.exp(bl)
        vb = _cols(proj, rows, O_V, O_V + D_BV).astype(BF16)
        for sg in range(gr // SUB_ROWS):
            sr = slice(sg * SUB_ROWS, (sg + 1) * SUB_ROWS)
            for p in range(H_B // 2):
                pc = slice(p * LANES, (p + 1) * LANES)
                vc = slice(p * 2 * DV_B, (p + 1) * 2 * DV_B)
                k_et = k_e[sr, pc].T.astype(BF16)
                k_lt = k_l[sr, pc].T
                v2 = vb[sr, vc]
                att = _dot(jnp.concatenate([q_even[sr, pc], q_odd[sr, pc]], axis=0), k_et)
                att_a = jnp.where(causal, att[0:SUB_ROWS], 0.0).astype(BF16)
                att_b = jnp.where(causal, att[SUB_ROWS:2 * SUB_ROWS], 0.0).astype(BF16)
                o_intra = jnp.concatenate(
                    [_dot(att_a, v2[:, 0:DV_B]), _dot(att_b, v2[:, DV_B:2 * DV_B])], axis=1)
                for c in range(SUB_ROWS // chunk):
                    c0 = sg * SUB_ROWS + c * chunk
                    row0 = g * gr + c0
                    n = row0 // tl
                    s_bd = st_scr[n, p]
                    o_inter = _dot(q_pair[c0:c0 + chunk, pc], s_bd.astype(BF16))
                    o_scr[row0:row0 + chunk, vc] = o_intra[c * chunk:(c + 1) * chunk] + o_inter
                    upd = _dot(jnp.where(in_chunk[c], k_lt, 0.0).astype(BF16), v2)
                    dcol = jnp.broadcast_to(dec[c0:c0 + 1, pc], (LANES, LANES)).T
                    st_scr[n, p] = (jnp.concatenate([dcol, dcol], axis=1) * s_bd
                                    + jnp.where(on_diag, upd, 0.0))

    for hh in range(H_B):
        vc = slice(hh * DV_B, (hh + 1) * DV_B)
        oh = o_scr[:, vc]
        ms = jnp.mean(oh * oh, axis=-1, keepdims=True)
        on = oh * lax.rsqrt(ms + EPS) * hg_ref[:, vc]
        yb = on * _silu(_cols(proj, every, O_BG + hh * DV_B, O_BG + (hh + 1) * DV_B))
        y_scr[:, D_A + hh * DV_B:D_A + (hh + 1) * DV_B] = yb.astype(BF16)

    y = _dot(y_scr[...], wout_ref[...])
    tr = slice(j * tl, (j + 1) * tl)
    xo_ref[:, tr, :] = x_ref[:, tr, :] + mod_ref[2] * y.reshape(nb, tl, D_MODEL)


def _even_kernel(x_ref, mod_ref, cst_ref, gst_ref, ng_ref, win_ref, cw_ref, cb_ref, lng_ref,
                 lnb_ref, gw2_ref, gb_ref, hg_ref, wout_ref, tri_ref, ones_ref,
                 xo_ref, co_ref, go_ref,
                 proj_scr, cbuf, obuf, hist_scr, st_scr, o_scr, y_scr, *, nb, tl, tps, chunk):
    step = pl.program_id(1)
    last = pl.num_programs(1) - 1

    @pl.when(step == 0)
    def _init():
        hist_scr[...] = jnp.zeros_like(hist_scr)
        hist_scr[:, SEG - CONV_PAD:SEG, :] = cst_ref[...]
        z = jnp.zeros((DK_B, DV_B), F32)
        for n in range(nb):
            for p in range(H_B // 2):
                top = jnp.concatenate([gst_ref[n, 2 * p], z], axis=1)
                bot = jnp.concatenate([z, gst_ref[n, 2 * p + 1]], axis=1)
                st_scr[n, p] = jnp.concatenate([top, bot], axis=0)

    conv = functools.partial(
        _even_conv, cw_ref=cw_ref, cb_ref=cb_ref, lng_ref=lng_ref, lnb_ref=lnb_ref, cbuf=cbuf,
        obuf=obuf, hist_scr=hist_scr, y_scr=y_scr, tl=tl)
    rest = functools.partial(
        _even_rest, x_ref=x_ref, mod_ref=mod_ref, gw2_ref=gw2_ref, gb_ref=gb_ref, hg_ref=hg_ref,
        wout_ref=wout_ref, tri_ref=tri_ref, ones_ref=ones_ref, xo_ref=xo_ref, st_scr=st_scr,
        o_scr=o_scr, y_scr=y_scr, nb=nb, tl=tl, chunk=chunk)
    _run_tiles(x_ref, mod_ref, ng_ref, win_ref, proj_scr, conv, rest, tl=tl, tps=tps)

    @pl.when(step == last)
    def _fin():
        co_ref[...] = hist_scr[:, SEG - CONV_PAD:SEG, :]
        for n in range(nb):
            for p in range(H_B // 2):
                go_ref[n, 2 * p] = st_scr[n, p, 0:DK_B, 0:DV_B]
                go_ref[n, 2 * p + 1] = st_scr[n, p, DK_B:2 * DK_B, DV_B:2 * DV_B]


def _decay_matrices(rows, chunk):
    i = np.arange(rows)
    same = (i[:, None] // chunk) == (i[None, :] // chunk)
    tri = same & (i[:, None] >= i[None, :])
    return jnp.asarray(tri, BF16), jnp.asarray(same, BF16)


def _even_call(x, mods, conv_st, gla_st, ng, w_in, cw, cb, lng, lnb, gw2, gb, hg, w_out, *,
               layer, mod_row0, state_layer, nb, tl, tps):
    b, l, d = x.shape
    e = layer // 2
    assert mod_row0 % nb == 0
    mod_blk0 = mod_row0 // nb
    chunk = min(GLA_CHUNK, l)
    m = nb * tl
    gr = min(m, GROUP_ROWS)
    n_tiles = l // tl
    assert b % nb == 0 and l % (tl * tps) == 0 and tl % chunk == 0 and m % gr == 0 and gr % chunk == 0
    assert tl % SEG == 0 and m % (SEG * SEGS) == 0
    assert gr % SUB_ROWS == 0 and SUB_ROWS % chunk == 0 and 2 * DK_B == LANES
    tri, ones_bd = _decay_matrices(gr, chunk)
    kern = functools.partial(_even_kernel, nb=nb, tl=tl, tps=tps, chunk=chunk)
    groups = m // (SEG * SEGS)
    return pl.pallas_call(
        kern,
        grid=(b // nb, n_tiles // tps),
        in_specs=[
            pl.BlockSpec((nb, tps * tl, d), lambda i, s: (i, s, 0)),
            pl.BlockSpec((None, 3, nb, 1, d), lambda i, s: (layer, 0, mod_blk0 + i, 0, 0)),
            pl.BlockSpec((None, nb, CONV_PAD, D_A), lambda i, s: (state_layer, i, 0, 0)),
            pl.BlockSpec((None, nb, H_B, DK_B, DV_B), lambda i, s: (state_layer, i, 0, 0, 0)),
            _layer_spec((1, d), layer),
            _layer_spec((d, EV_IN_PAD), e),
            _layer_spec((CONV_W, D_A), e),
            _layer_spec((1, D_A), e),
            _layer_spec((1, D_A), e),
            _layer_spec((1, D_A), e),
            _layer_spec((LANES, D_BK), e),
            _layer_spec((1, D_BK), e),
            _layer_spec((1, D_BV), e),
            _layer_spec((D_A + D_BV, d), e),
            _const_spec((gr, gr)),
            _const_spec((gr, gr)),
        ],
        out_specs=[
            pl.BlockSpec((nb, tps * tl, d), lambda i, s: (i, s, 0)),
            pl.BlockSpec((nb, CONV_PAD, D_A), lambda i, s: (i, 0, 0)),
            pl.BlockSpec((nb, H_B, DK_B, DV_B), lambda i, s: (i, 0, 0, 0)),
        ],
        out_shape=[
            jax.ShapeDtypeStruct((b, l, d), F32),
            jax.ShapeDtypeStruct((b, CONV_PAD, D_A), F32),
            jax.ShapeDtypeStruct((b, H_B, DK_B, DV_B), F32),
        ],
        scratch_shapes=[
            pltpu.VMEM((min(tps, 2), -(-EV_IN_PAD // MXU_COLS), m, MXU_COLS), F32),
            pltpu.VMEM((groups, N_SLAB, SEGS * SEG_PITCH, LANES), F32),
            pltpu.VMEM((groups, N_SLAB, SEGS * OUT_PITCH, LANES), F32),
            pltpu.VMEM((nb, SEG, D_A), F32),
            pltpu.VMEM((nb, H_B // 2, LANES, 2 * DV_B), F32),
            pltpu.VMEM((m, D_BV), F32),
            pltpu.VMEM((m, D_A + D_BV), BF16),
        ],
        compiler_params=pltpu.CompilerParams(
            dimension_semantics=("arbitrary", "arbitrary"), vmem_limit_bytes=VMEM_LIMIT),
        name="even_layer",
    )(x, mods, conv_st, gla_st, ng, w_in, cw, cb, lng, lnb, gw2, gb, hg, w_out, tri, ones_bd)


def _odd_pool(proj, j, g, vbuf, pbuf, phist_scr, p_scr, *, tl, tps, pos0):
    tile_pos = pos0 + (pl.program_id(1) * tps + j) * tl
    segs_per_seq = tl // SEG
    prev = None
    for i in range(SEGS):
        gi = g * SEGS + i
        n, si = divmod(gi, segs_per_seq)
        blk = _cols(proj, slice(gi * SEG, (gi + 1) * SEG), 0, D_C)
        halo = phist_scr[n] if prev is None or si == 0 else prev[SEG - POOL_HIST:SEG, :]
        for s in range(P_SLAB):
            ls = slice(s * LANES, (s + 1) * LANES)
            vbuf[g, s,i * POOL_PITCH:i * POOL_PITCH + POOL_HIST, :] = halo[:, ls]
            vbuf[g, s,i * POOL_PITCH + POOL_HIST:i * POOL_PITCH + POOL_HIST + SEG, :] = blk[:, ls]
        if i == SEGS - 1 or si == segs_per_seq - 1:
            phist_scr[n] = blk[SEG - POOL_HIST:SEG, :]
        prev = blk

    sub = lax.broadcasted_iota(jnp.int32, (SEGS, LANES), 0)
    seg_pos1 = tile_pos + 1 + ((sub + g * SEGS) % segs_per_seq) * SEG
    for wi, w in enumerate(POOL_WINDOWS):
        for s in range(wi * (POOL_GW // LANES), (wi + 1) * (POOL_GW // LANES)):
            lo = -(w - 1)
            cur = {k: vbuf[g, s,pl.ds(POOL_HIST + k, SEGS, stride=POOL_PITCH), :]
                   for k in range(lo, SEG)}
            xk = cur
            d = 1
            while d < w:
                lo += d
                cur = {k: cur[k] + cur[k - d] for k in range(lo, SEG)}
                d *= 2
            for k in range(SEG):
                if k >= w - 1:
                    p = cur[k] * (1.0 / w) - xk[k]
                else:
                    cnt = jnp.minimum(seg_pos1 + k, w).astype(F32)
                    p = cur[k] / cnt - xk[k]
                pbuf[g, s, pl.ds(k, SEGS, stride=OUT_PITCH), :] = p

    for i in range(SEGS):
        gi = g * SEGS + i
        rows = slice(gi * SEG, (gi + 1) * SEG)
        for s in range(P_SLAB):
            p_scr[rows, s * LANES:(s + 1) * LANES] = pbuf[
                g, s, i * OUT_PITCH:i * OUT_PITCH + SEG, :].astype(BF16)


def _odd_rest(proj, j, x_ref, mod_ref, gw_ref, gb_ref, sc_ref, wout_ref, fg_ref, xo_ref,
              p_scr, z_scr, *, nb, tl, final):
    for gi in range(len(POOL_WINDOWS)):
        cs = slice(gi * POOL_GW, (gi + 1) * POOL_GW)
        pg = _dot(p_scr[:, cs], gw_ref[gi]) + gb_ref[:, cs]
        z = pg * sc_ref[:, cs] * _silu(
            _cols(proj, slice(None), D_C + gi * POOL_GW, D_C + (gi + 1) * POOL_GW))
        z_scr[:, cs] = z.astype(BF16)

    y = _dot(z_scr[...], wout_ref[...])
    tr = slice(j * tl, (j + 1) * tl)
    xo = x_ref[:, tr, :] + mod_ref[2] * y.reshape(nb, tl, D_MODEL)
    if final:
        ms = jnp.mean(xo * xo, axis=-1, keepdims=True)
        xo = xo * lax.rsqrt(ms + EPS) * fg_ref[...]
    xo_ref[:, tr, :] = xo


def _odd_kernel(x_ref, mod_ref, pst_ref, ng_ref, win_ref, gw_ref, gb_ref, sc_ref, wout_ref, fg_ref,
                xo_ref, po_ref,
                proj_scr, vbuf, pbuf, phist_scr, p_scr, z_scr, *, nb, tl, tps, pos0, final):
    step = pl.program_id(1)
    last = pl.num_programs(1) - 1

    @pl.when(step == 0)
    def _init():
        phist_scr[...] = jnp.zeros_like(phist_scr)
        phist_scr[:, POOL_HIST - POOL_PAD:POOL_HIST, :] = pst_ref[...]

    pool = functools.partial(
        _odd_pool, vbuf=vbuf, pbuf=pbuf, phist_scr=phist_scr, p_scr=p_scr, tl=tl, tps=tps, pos0=pos0)
    rest = functools.partial(
        _odd_rest, x_ref=x_ref, mod_ref=mod_ref, gw_ref=gw_ref, gb_ref=gb_ref, sc_ref=sc_ref,
        wout_ref=wout_ref, fg_ref=fg_ref, xo_ref=xo_ref, p_scr=p_scr, z_scr=z_scr,
        nb=nb, tl=tl, final=final)
    _run_tiles(x_ref, mod_ref, ng_ref, win_ref, proj_scr, pool, rest, tl=tl, tps=tps)

    @pl.when(step == last)
    def _fin():
        po_ref[...] = phist_scr[:, POOL_HIST - POOL_PAD:POOL_HIST, :]


def _odd_call(x, mods, pool_st, ng, w_in, gw, gb, sc, w_out, fg, *,
              layer, mod_row0, state_layer, nb, tl, tps, pos0):
    b, l, d = x.shape
    o = layer // 2
    final = layer == DEPTH - 1
    assert mod_row0 % nb == 0
    mod_blk0 = mod_row0 // nb
    m = nb * tl
    n_tiles = l // tl
    assert b % nb == 0 and l % (tl * tps) == 0 and tl % SEG == 0 and m % (SEG * SEGS) == 0
    kern = functools.partial(_odd_kernel, nb=nb, tl=tl, tps=tps, pos0=pos0, final=final)
    groups = m // (SEG * SEGS)
    return pl.pallas_call(
        kern,
        grid=(b // nb, n_tiles // tps),
        in_specs=[
            pl.BlockSpec((nb, tps * tl, d), lambda i, s: (i, s, 0)),
            pl.BlockSpec((None, 3, nb, 1, d), lambda i, s: (layer, 0, mod_blk0 + i, 0, 0)),
            pl.BlockSpec((None, nb, POOL_PAD, D_C), lambda i, s: (state_layer, i, 0, 0)),
            _layer_spec((1, d), layer),
            _layer_spec((d, 2 * D_C), o),
            _layer_spec((len(POOL_WINDOWS), POOL_GW, POOL_GW), o),
            _layer_spec((1, D_C), o),
            _layer_spec((1, D_C), o),
            _layer_spec((D_C, d), o),
            _const_spec((1, d)),
        ],
        out_specs=[
            pl.BlockSpec((nb, tps * tl, d), lambda i, s: (i, s, 0)),
            pl.BlockSpec((nb, POOL_PAD, D_C), lambda i, s: (i, 0, 0)),
        ],
        out_shape=[
            jax.ShapeDtypeStruct((b, l, d), F32),
            jax.ShapeDtypeStruct((b, POOL_PAD, D_C), F32),
        ],
        scratch_shapes=[
            pltpu.VMEM((min(tps, 2), 2 * D_C // MXU_COLS, m, MXU_COLS), F32),
            pltpu.VMEM((groups, P_SLAB, SEGS * POOL_PITCH, LANES), F32),
            pltpu.VMEM((groups, P_SLAB, SEGS * OUT_PITCH, LANES), F32),
            pltpu.VMEM((nb, POOL_HIST, D_C), F32),
            pltpu.VMEM((m, D_C), BF16),
            pltpu.VMEM((m, D_C), BF16),
        ],
        compiler_params=pltpu.CompilerParams(
            dimension_semantics=("arbitrary", "arbitrary"), vmem_limit_bytes=VMEM_LIMIT),
        name="odd_layer",
    )(x, mods, pool_st, ng, w_in, gw, gb, sc, w_out, fg)


PROMPT_TILE = dict(nb=2, tl=256, tps=1)
SAMPLE_TILE = dict(nb=8, tl=32, tps=1)


def kernel(x_prompt, x_sample, c_prompt, c_sample, state_conv, state_gla, state_pool, norm_g, ada_w, ada_b, ev_w_in, ev_conv_w, ev_conv_b, ev_ln_g, ev_ln_b, ev_gate_w2, ev_gate_b, ev_head_g, ev_w_out, od_w_in, od_group_w, od_group_b, od_scale, od_w_out, final_g):
    b_p = x_prompt.shape[0]
    b_s = x_sample.shape[0]
    n_even = ev_w_in.shape[0]
    n_odd = od_w_in.shape[0]

    c_all = jnp.concatenate([c_sample, c_prompt], axis=0)
    bp = -(-c_all.shape[0] // SUBLANES) * SUBLANES
    c_all = jnp.pad(c_all, ((0, bp - c_all.shape[0]), (0, 0)))
    mods = _ada_call(c_all, ada_w, ada_b).reshape(DEPTH, 3, bp, 1, D_MODEL)

    ev_w_in_b = jnp.pad(ev_w_in, ((0, 0), (0, 0), (0, EV_IN_PAD - EV_IN))).astype(BF16)
    ev_gw2_b = jnp.pad(ev_gate_w2, ((0, 0), (0, LANES - GATE_RANK), (0, 0))).astype(BF16)
    ev_w_out_b = ev_w_out.astype(BF16)
    od_w_in_b = od_w_in.astype(BF16)
    od_gw_b = od_group_w.astype(BF16)
    od_w_out_b = od_w_out.astype(BF16)

    zero_conv = jnp.zeros((1, b_p, CONV_PAD, D_A), F32)
    zero_gla = jnp.zeros((1, b_p, H_B, DK_B, DV_B), F32)
    zero_pool = jnp.zeros((1, b_p, POOL_PAD, D_C), F32)

    def rows(a):
        return a.reshape(a.shape[0], 1, a.shape[1])

    ng = rows(norm_g)
    ev_w = (ng, ev_w_in_b, ev_conv_w, rows(ev_conv_b), rows(ev_ln_g), rows(ev_ln_b), ev_gw2_b,
            rows(ev_gate_b), rows(ev_head_g), ev_w_out_b)
    od_w = (ng, od_w_in_b, od_gw_b, rows(od_group_b), rows(od_scale), od_w_out_b,
            final_g.reshape(1, D_MODEL))
    prompt = dict(mod_row0=b_s, state_layer=0, **PROMPT_TILE)

    xp, xs = x_prompt, x_sample
    conv_p, gla_p, pool_p, conv_s, gla_s, pool_s = [], [], [], [], [], []
    for l in range(DEPTH):
        sample = dict(mod_row0=0, state_layer=l // 2, **SAMPLE_TILE)
        if l % 2 == 0:
            xp, cp, gp = _even_call(xp, mods, zero_conv, zero_gla, *ev_w, layer=l, **prompt)
            xs, cs, gs = _even_call(xs, mods, state_conv, state_gla, *ev_w, layer=l, **sample)
            conv_p.append(cp)
            gla_p.append(gp)
            conv_s.append(cs)
            gla_s.append(gs)
        else:
            xp, pp = _odd_call(xp, mods, zero_pool, *od_w, layer=l, pos0=0, **prompt)
            xs, ps = _odd_call(xs, mods, state_pool, *od_w, layer=l, pos0=PAST_LEN, **sample)
            pool_p.append(pp)
            pool_s.append(ps)
    assert n_even == len(conv_p) and n_odd == len(pool_p)
    return (xp, xs, jnp.stack(conv_p), jnp.stack(gla_p), jnp.stack(pool_p),
            jnp.stack(conv_s), jnp.stack(gla_s), jnp.stack(pool_s))
```

```python
import functools

import numpy as np
import jax
import jax.numpy as jnp
from jax import lax
from jax.experimental import pallas as pl
from jax.experimental.pallas import tpu as pltpu

F32 = jnp.float32
BF16 = jnp.bfloat16

LANES = 128
SUBLANES = 8
VMEM_LIMIT = 56 * 1024 * 1024
MXU_COLS = 256
GROUP_ROWS = MXU_COLS

D_MODEL = 1024
DEPTH = 4
EPS = 1e-6
PAST_LEN = 2048
LOG2E = 1.4426950408889634

D_A = D_MODEL // 2
CONV_W = 31
CONV_PAD = CONV_W - 1
SEG = 32
SEGS = SUBLANES
SEG_PITCH = 68
OUT_PITCH = 36
N_SLAB = D_A // LANES
CONV_KB = 4

H_B = 4
DK_B = D_MODEL // 16
DV_B = D_MODEL // 8
D_BK = H_B * DK_B
D_BV = H_B * DV_B
GATE_RANK = 16
GATE_TAU = 16.0
GLA_CHUNK = 64
SUB_ROWS = LANES

D_C = D_MODEL
POOL_WINDOWS = (2, 4, 8, 16)
POOL_GW = D_C // len(POOL_WINDOWS)
POOL_PAD = 15
POOL_HIST = 16
POOL_PITCH = 52
P_SLAB = D_C // LANES

O_VAL, O_GLU, O_GATE = 0, D_A, 2 * D_A
O_Q = 3 * D_A
O_K = O_Q + D_BK
O_V = O_K + D_BK
O_BG = O_V + D_BV
O_LR = O_BG + D_BV
EV_IN = O_LR + GATE_RANK
EV_IN_PAD = O_LR + LANES


def _sigmoid(x):
    return 1.0 / (1.0 + jnp.exp2(x * (-LOG2E)))


def _silu(x):
    return x * _sigmoid(x)


def _split_bf16(x):
    hi = x.astype(BF16)
    lo = (x - hi.astype(F32)).astype(BF16)
    return hi, lo


def _dot(a, b):
    return jnp.dot(a, b, preferred_element_type=F32)


def _const_spec(shape):
    nd = len(shape)
    return pl.BlockSpec(shape, lambda *_: (0,) * nd, pipeline_mode=pl.Buffered(1))


def _layer_spec(shape, layer):
    nd = len(shape)
    return pl.BlockSpec((None,) + tuple(shape), lambda *_: (layer,) + (0,) * nd,
                        pipeline_mode=pl.Buffered(1))


def _ada_kernel(c_ref, w_ref, b_ref, o_ref):
    cs = _silu(c_ref[...])
    o_ref[0, 0] = _dot(cs.astype(BF16), w_ref[0].astype(BF16)) + b_ref[0, 0]


def _ada_call(c_all, ada_w, ada_b):
    bp = c_all.shape[0]
    return pl.pallas_call(
        _ada_kernel,
        grid=(DEPTH, 3),
        in_specs=[
            pl.BlockSpec((bp, D_MODEL), lambda l, j: (0, 0)),
            pl.BlockSpec((1, D_MODEL, D_MODEL), lambda l, j: (l, 0, j)),
            pl.BlockSpec((1, 1, 1, D_MODEL), lambda l, j: (l, j, 0, 0)),
        ],
        out_specs=pl.BlockSpec((1, 1, bp, D_MODEL), lambda l, j: (l, j, 0, 0)),
        out_shape=jax.ShapeDtypeStruct((DEPTH, 3, bp, D_MODEL), F32),
        compiler_params=pltpu.CompilerParams(
            dimension_semantics=("arbitrary", "arbitrary"), vmem_limit_bytes=VMEM_LIMIT),
        name="ada_mod",
    )(c_all, ada_w, ada_b.reshape(DEPTH, 3, 1, D_MODEL))


def _modulated_input(x3, mod_ref, ng_ref):
    nb, tl, d = x3.shape
    shift = mod_ref[0]
    scale = mod_ref[1]
    ms = jnp.mean(x3 * x3, axis=-1, keepdims=True)
    gs = ng_ref[...] * (1.0 + scale)
    h = (x3 * lax.rsqrt(ms + EPS)) * gs + shift
    return h.astype(BF16).reshape(nb * tl, d)


def _cols(proj, rows, c0, c1):
    parts = []
    while c0 < c1:
        p, off = divmod(c0, MXU_COLS)
        w = min(MXU_COLS - off, c1 - c0)
        parts.append(proj[p, rows, off:off + w])
        c0 += w
    return parts[0] if len(parts) == 1 else jnp.concatenate(parts, axis=1)


def _run_tiles(x_ref, mod_ref, ng_ref, win_ref, proj_scr, group_stage, rest_stage, *, tl, tps):
    width = win_ref.shape[1]
    for j in range(tps):
        hb = _modulated_input(x_ref[:, j * tl:(j + 1) * tl, :], mod_ref, ng_ref)
        proj = proj_scr.at[j % 2]
        groups = hb.shape[0] // GROUP_ROWS
        for g in range(groups):
            rows = slice(g * GROUP_ROWS, (g + 1) * GROUP_ROWS)
            for p, c0 in enumerate(range(0, width, MXU_COLS)):
                c1 = min(c0 + MXU_COLS, width)
                proj[p, rows, 0:c1 - c0] = _dot(hb[rows], win_ref[:, c0:c1])
        for g in range(groups):
            group_stage(proj, j, g)
        rest_stage(proj, j)


def _even_conv(proj, j, g, cw_ref, cb_ref, lng_ref, lnb_ref, cbuf, obuf, hist_scr, y_scr,
               *, tl):
    del j
    segs_per_seq = tl // SEG
    prev = None
    for i in range(SEGS):
        gi = g * SEGS + i
        n, si = divmod(gi, segs_per_seq)
        rows = slice(gi * SEG, (gi + 1) * SEG)
        blk = _cols(proj, rows, O_VAL, O_VAL + D_A) * _sigmoid(_cols(proj, rows, O_GLU, O_GLU + D_A))
        halo = hist_scr[n] if prev is None or si == 0 else prev
        for s in range(N_SLAB):
            ls = slice(s * LANES, (s + 1) * LANES)
            cbuf[g, s,i * SEG_PITCH:i * SEG_PITCH + SEG, :] = halo[:, ls]
            cbuf[g, s,i * SEG_PITCH + SEG:i * SEG_PITCH + 2 * SEG, :] = blk[:, ls]
        if i == SEGS - 1 or si == segs_per_seq - 1:
            hist_scr[n] = blk
        prev = blk

    inv_da = 1.0 / D_A
    for kb in range(SEG // CONV_KB):
        acc = [[jnp.broadcast_to(cb_ref[:, s * LANES:(s + 1) * LANES], (SEGS, LANES))
                for s in range(N_SLAB)] for _ in range(CONV_KB)]
        for jt in range(CONV_W):
            for s in range(N_SLAB):
                wj = cw_ref[jt:jt + 1, s * LANES:(s + 1) * LANES]
                for kk in range(CONV_KB):
                    start = SEG - CONV_PAD + kb * CONV_KB + kk + jt
                    acc[kk][s] = acc[kk][s] + wj * cbuf[g, s, pl.ds(start, SEGS, stride=SEG_PITCH), :]
        for kk in range(CONV_KB):
            a = acc[kk]
            mu = jnp.sum(a[0] + a[1] + a[2] + a[3], axis=-1, keepdims=True) * inv_da
            xc = [v - mu for v in a]
            sq = xc[0] * xc[0] + xc[1] * xc[1] + xc[2] * xc[2] + xc[3] * xc[3]
            rstd = lax.rsqrt(jnp.sum(sq, axis=-1, keepdims=True) * inv_da + EPS)
            for s in range(N_SLAB):
                ls = slice(s * LANES, (s + 1) * LANES)
                yn = xc[s] * rstd * lng_ref[:, ls] + lnb_ref[:, ls]
                obuf[g, s, pl.ds(kb * CONV_KB + kk, SEGS, stride=OUT_PITCH), :] = _silu(yn)

    for i in range(SEGS):
        gi = g * SEGS + i
        rows = slice(gi * SEG, (gi + 1) * SEG)
        for s in range(N_SLAB):
            ya = obuf[g, s,i * OUT_PITCH:i * OUT_PITCH + SEG, :] * _silu(
                _cols(proj, rows, O_GATE + s * LANES, O_GATE + (s + 1) * LANES))
            y_scr[rows, s * LANES:(s + 1) * LANES] = ya.astype(BF16)


def _even_rest(proj, j, x_ref, mod_ref, gw2_ref, gb_ref, hg_ref, wout_ref, tri_ref, ones_ref,
               xo_ref, st_scr, o_scr, y_scr, *, nb, tl, chunk):
    m = nb * tl
    gr = min(m, GROUP_ROWS)
    lane = lax.broadcasted_iota(jnp.int32, (1, LANES), 1)
    lo_half = lane < DK_B

    every = slice(None)
    pre = _dot(_cols(proj, every, O_LR, O_LR + LANES).astype(BF16), gw2_ref[...]) + gb_ref[...]
    la = (jnp.minimum(pre, 0.0) - jnp.log1p(jnp.exp(-jnp.abs(pre)))) * (1.0 / GATE_TAU)
    la_hi, la_lo = _split_bf16(la)
    tri = tri_ref[...]
    ones_bd = ones_ref[...]
    causal = tri_ref[0:SUB_ROWS, 0:SUB_ROWS].astype(F32) > 0.5
    pair_lo = jnp.concatenate([lo_half, lo_half], axis=1)
    srow = lax.broadcasted_iota(jnp.int32, (LANES, 2 * DV_B), 0)
    scol = lax.broadcasted_iota(jnp.int32, (LANES, 2 * DV_B), 1)
    on_diag = (srow < DK_B) == (scol < DV_B)
    in_chunk = [(lane >= c * chunk) & (lane < (c + 1) * chunk) for c in range(SUB_ROWS // chunk)]
    for g in range(m // gr):
        rows = slice(g * gr, (g + 1) * gr)
        b = _dot(tri, la_hi[rows]) + _dot(tri, la_lo[rows])
        bl = _dot(ones_bd, la_hi[rows]) + _dot(ones_bd, la_lo[rows])
        k = _cols(proj, rows, O_K, O_K + D_BK)
        q_e = _cols(proj, rows, O_Q, O_Q + D_BK) * (DK_B ** -0.5) * jnp.exp(b)
        q_pair = q_e.astype(BF16)
        q_even = jnp.where(pair_lo, q_e, 0.0).astype(BF16)
        q_odd = jnp.where(pair_lo, 0.0, q_e).astype(BF16)
        k_e = k * jnp.exp(-b)
        k_l = k * jnp.exp(bl - b)
        dec = jnp.exp(bl)
        vb = _cols(proj, rows, O_V, O_V + D_BV).astype(BF16)
        for sg in range(gr // SUB_ROWS):
            sr = slice(sg * SUB_ROWS, (sg + 1) * SUB_ROWS)
            for p in range(H_B // 2):
                pc = slice(p * LANES, (p + 1) * LANES)
                vc = slice(p * 2 * DV_B, (p + 1) * 2 * DV_B)
                k_et = k_e[sr, pc].T.astype(BF16)
                k_lt = k_l[sr, pc].T
                v2 = vb[sr, vc]
                att = _dot(jnp.concatenate([q_even[sr, pc], q_odd[sr, pc]], axis=0), k_et)
                att_a = jnp.where(causal, att[0:SUB_ROWS], 0.0).astype(BF16)
                att_b = jnp.where(causal, att[SUB_ROWS:2 * SUB_ROWS], 0.0).astype(BF16)
                o_intra = jnp.concatenate(
                    [_dot(att_a, v2[:, 0:DV_B]), _dot(att_b, v2[:, DV_B:2 * DV_B])], axis=1)
                for c in range(SUB_ROWS // chunk):
                    c0 = sg * SUB_ROWS + c * chunk
                    row0 = g * gr + c0
                    n = row0 // tl
                    s_bd = st_scr[n, p]
                    o_inter = _dot(q_pair[c0:c0 + chunk, pc], s_bd.astype(BF16))
                    o_scr[row0:row0 + chunk, vc] = o_intra[c * chunk:(c + 1) * chunk] + o_inter
                    upd = _dot(jnp.where(in_chunk[c], k_lt, 0.0).astype(BF16), v2)
                    dcol = jnp.broadcast_to(dec[c0:c0 + 1, pc], (LANES, LANES)).T
                    st_scr[n, p] = (jnp.concatenate([dcol, dcol], axis=1) * s_bd
                                    + jnp.where(on_diag, upd, 0.0))

    for hh in range(H_B):
        vc = slice(hh * DV_B, (hh + 1) * DV_B)
        oh = o_scr[:, vc]
        ms = jnp.mean(oh * oh, axis=-1, keepdims=True)
        on = oh * lax.rsqrt(ms + EPS) * hg_ref[:, vc]
        yb = on * _silu(_cols(proj, every, O_BG + hh * DV_B, O_BG + (hh + 1) * DV_B))
        y_scr[:, D_A + hh * DV_B:D_A + (hh + 1) * DV_B] = yb.astype(BF16)

    y = _dot(y_scr[...], wout_ref[...])
    tr = slice(j * tl, (j + 1) * tl)
    xo_ref[:, tr, :] = x_ref[:, tr, :] + mod_ref[2] * y.reshape(nb, tl, D_MODEL)


def _even_kernel(x_ref, mod_ref, cst_ref, gst_ref, ng_ref, win_ref, cw_ref, cb_ref, lng_ref,
                 lnb_ref, gw2_ref, gb_ref, hg_ref, wout_ref, tri_ref, ones_ref,
                 xo_ref, co_ref, go_ref,
                 proj_scr, cbuf, obuf, hist_scr, st_scr, o_scr, y_scr, *, nb, tl, tps, chunk):
    step = pl.program_id(1)
    last = pl.num_programs(1) - 1

    @pl.when(step == 0)
    def _init():
        hist_scr[...] = jnp.zeros_like(hist_scr)
        hist_scr[:, SEG - CONV_PAD:SEG, :] = cst_ref[...]
        z = jnp.zeros((DK_B, DV_B), F32)
        for n in range(nb):
            for p in range(H_B // 2):
                top = jnp.concatenate([gst_ref[n, 2 * p], z], axis=1)
                bot = jnp.concatenate([z, gst_ref[n, 2 * p + 1]], axis=1)
                st_scr[n, p] = jnp.concatenate([top, bot], axis=0)

    conv = functools.partial(
        _even_conv, cw_ref=cw_ref, cb_ref=cb_ref, lng_ref=lng_ref, lnb_ref=lnb_ref, cbuf=cbuf,
        obuf=obuf, hist_scr=hist_scr, y_scr=y_scr, tl=tl)
    rest = functools.partial(
        _even_rest, x_ref=x_ref, mod_ref=mod_ref, gw2_ref=gw2_ref, gb_ref=gb_ref, hg_ref=hg_ref,
        wout_ref=wout_ref, tri_ref=tri_ref, ones_ref=ones_ref, xo_ref=xo_ref, st_scr=st_scr,
        o_scr=o_scr, y_scr=y_scr, nb=nb, tl=tl, chunk=chunk)
    _run_tiles(x_ref, mod_ref, ng_ref, win_ref, proj_scr, conv, rest, tl=tl, tps=tps)

    @pl.when(step == last)
    def _fin():
        co_ref[...] = hist_scr[:, SEG - CONV_PAD:SEG, :]
        for n in range(nb):
            for p in range(H_B // 2):
                go_ref[n, 2 * p] = st_scr[n, p, 0:DK_B, 0:DV_B]
                go_ref[n, 2 * p + 1] = st_scr[n, p, DK_B:2 * DK_B, DV_B:2 * DV_B]


def _decay_matrices(rows, chunk):
    i = np.arange(rows)
    same = (i[:, None] // chunk) == (i[None, :] // chunk)
    tri = same & (i[:, None] >= i[None, :])
    return jnp.asarray(tri, BF16), jnp.asarray(same, BF16)


def _even_call(x, mods, conv_st, gla_st, ng, w_in, cw, cb, lng, lnb, gw2, gb, hg, w_out, *,
               layer, mod_row0, state_layer, nb, tl, tps):
    b, l, d = x.shape
    e = layer // 2
    assert mod_row0 % nb == 0
    mod_blk0 = mod_row0 // nb
    chunk = min(GLA_CHUNK, l)
    m = nb * tl
    gr = min(m, GROUP_ROWS)
    n_tiles = l // tl
    assert b % nb == 0 and l % (tl * tps) == 0 and tl % chunk == 0 and m % gr == 0 and gr % chunk == 0
    assert tl % SEG == 0 and m % (SEG * SEGS) == 0
    assert gr % SUB_ROWS == 0 and SUB_ROWS % chunk == 0 and 2 * DK_B == LANES
    tri, ones_bd = _decay_matrices(gr, chunk)
    kern = functools.partial(_even_kernel, nb=nb, tl=tl, tps=tps, chunk=chunk)
    groups = m // (SEG * SEGS)
    return pl.pallas_call(
        kern,
        grid=(b // nb, n_tiles // tps),
        in_specs=[
            pl.BlockSpec((nb, tps * tl, d), lambda i, s: (i, s, 0)),
            pl.BlockSpec((None, 3, nb, 1, d), lambda i, s: (layer, 0, mod_blk0 + i, 0, 0)),
            pl.BlockSpec((None, nb, CONV_PAD, D_A), lambda i, s: (state_layer, i, 0, 0)),
            pl.BlockSpec((None, nb, H_B, DK_B, DV_B), lambda i, s: (state_layer, i, 0, 0, 0)),
            _layer_spec((1, d), layer),
            _layer_spec((d, EV_IN_PAD), e),
            _layer_spec((CONV_W, D_A), e),
            _layer_spec((1, D_A), e),
            _layer_spec((1, D_A), e),
            _layer_spec((1, D_A), e),
            _layer_spec((LANES, D_BK), e),
            _layer_spec((1, D_BK), e),
            _layer_spec((1, D_BV), e),
            _layer_spec((D_A + D_BV, d), e),
            _const_spec((gr, gr)),
            _const_spec((gr, gr)),
        ],
        out_specs=[
            pl.BlockSpec((nb, tps * tl, d), lambda i, s: (i, s, 0)),
            pl.BlockSpec((nb, CONV_PAD, D_A), lambda i, s: (i, 0, 0)),
            pl.BlockSpec((nb, H_B, DK_B, DV_B), lambda i, s: (i, 0, 0, 0)),
        ],
        out_shape=[
            jax.ShapeDtypeStruct((b, l, d), F32),
            jax.ShapeDtypeStruct((b, CONV_PAD, D_A), F32),
            jax.ShapeDtypeStruct((b, H_B, DK_B, DV_B), F32),
        ],
        scratch_shapes=[
            pltpu.VMEM((min(tps, 2), -(-EV_IN_PAD // MXU_COLS), m, MXU_COLS), F32),
            pltpu.VMEM((groups, N_SLAB, SEGS * SEG_PITCH, LANES), F32),
            pltpu.VMEM((groups, N_SLAB, SEGS * OUT_PITCH, LANES), F32),
            pltpu.VMEM((nb, SEG, D_A), F32),
            pltpu.VMEM((nb, H_B // 2, LANES, 2 * DV_B), F32),
            pltpu.VMEM((m, D_BV), F32),
            pltpu.VMEM((m, D_A + D_BV), BF16),
        ],
        compiler_params=pltpu.CompilerParams(
            dimension_semantics=("arbitrary", "arbitrary"), vmem_limit_bytes=VMEM_LIMIT),
        name="even_layer",
    )(x, mods, conv_st, gla_st, ng, w_in, cw, cb, lng, lnb, gw2, gb, hg, w_out, tri, ones_bd)


def _odd_pool(proj, j, g, vbuf, pbuf, phist_scr, p_scr, *, tl, tps, pos0):
    tile_pos = pos0 + (pl.program_id(1) * tps + j) * tl
    segs_per_seq = tl // SEG
    prev = None
    for i in range(SEGS):
        gi = g * SEGS + i
        n, si = divmod(gi, segs_per_seq)
        blk = _cols(proj, slice(gi * SEG, (gi + 1) * SEG), 0, D_C)
        halo = phist_scr[n] if prev is None or si == 0 else prev[SEG - POOL_HIST:SEG, :]
        for s in range(P_SLAB):
            ls = slice(s * LANES, (s + 1) * LANES)
            vbuf[g, s,i * POOL_PITCH:i * POOL_PITCH + POOL_HIST, :] = halo[:, ls]
            vbuf[g, s,i * POOL_PITCH + POOL_HIST:i * POOL_PITCH + POOL_HIST + SEG, :] = blk[:, ls]
        if i == SEGS - 1 or si == segs_per_seq - 1:
            phist_scr[n] = blk[SEG - POOL_HIST:SEG, :]
        prev = blk

    sub = lax.broadcasted_iota(jnp.int32, (SEGS, LANES), 0)
    seg_pos1 = tile_pos + 1 + ((sub + g * SEGS) % segs_per_seq) * SEG
    for wi, w in enumerate(POOL_WINDOWS):
        for s in range(wi * (POOL_GW // LANES), (wi + 1) * (POOL_GW // LANES)):
            lo = -(w - 1)
            cur = {k: vbuf[g, s,pl.ds(POOL_HIST + k, SEGS, stride=POOL_PITCH), :]
                   for k in range(lo, SEG)}
            xk = cur
            d = 1
            while d < w:
                lo += d
                cur = {k: cur[k] + cur[k - d] for k in range(lo, SEG)}
                d *= 2
            for k in range(SEG):
                if k >= w - 1:
                    p = cur[k] * (1.0 / w) - xk[k]
                else:
                    cnt = jnp.minimum(seg_pos1 + k, w).astype(F32)
                    p = cur[k] / cnt - xk[k]
                pbuf[g, s, pl.ds(k, SEGS, stride=OUT_PITCH), :] = p

    for i in range(SEGS):
        gi = g * SEGS + i
        rows = slice(gi * SEG, (gi + 1) * SEG)
        for s in range(P_SLAB):
            p_scr[rows, s * LANES:(s + 1) * LANES] = pbuf[
                g, s, i * OUT_PITCH:i * OUT_PITCH + SEG, :].astype(BF16)


def _odd_rest(proj, j, x_ref, mod_ref, gw_ref, gb_ref, sc_ref, wout_ref, fg_ref, xo_ref,
              p_scr, z_scr, *, nb, tl, final):
    for gi in range(len(POOL_WINDOWS)):
        cs = slice(gi * POOL_GW, (gi + 1) * POOL_GW)
        pg = _dot(p_scr[:, cs], gw_ref[gi]) + gb_ref[:, cs]
        z = pg * sc_ref[:, cs] * _silu(
            _cols(proj, slice(None), D_C + gi * POOL_GW, D_C + (gi + 1) * POOL_GW))
        z_scr[:, cs] = z.astype(BF16)

    y = _dot(z_scr[...], wout_ref[...])
    tr = slice(j * tl, (j + 1) * tl)
    xo = x_ref[:, tr, :] + mod_ref[2] * y.reshape(nb, tl, D_MODEL)
    if final:
        ms = jnp.mean(xo * xo, axis=-1, keepdims=True)
        xo = xo * lax.rsqrt(ms + EPS) * fg_ref[...]
    xo_ref[:, tr, :] = xo


def _odd_kernel(x_ref, mod_ref, pst_ref, ng_ref, win_ref, gw_ref, gb_ref, sc_ref, wout_ref, fg_ref,
                xo_ref, po_ref,
                proj_scr, vbuf, pbuf, phist_scr, p_scr, z_scr, *, nb, tl, tps, pos0, final):
    step = pl.program_id(1)
    last = pl.num_programs(1) - 1

    @pl.when(step == 0)
    def _init():
        phist_scr[...] = jnp.zeros_like(phist_scr)
        phist_scr[:, POOL_HIST - POOL_PAD:POOL_HIST, :] = pst_ref[...]

    pool = functools.partial(
        _odd_pool, vbuf=vbuf, pbuf=pbuf, phist_scr=phist_scr, p_scr=p_scr, tl=tl, tps=tps, pos0=pos0)
    rest = functools.partial(
        _odd_rest, x_ref=x_ref, mod_ref=mod_ref, gw_ref=gw_ref, gb_ref=gb_ref, sc_ref=sc_ref,
        wout_ref=wout_ref, fg_ref=fg_ref, xo_ref=xo_ref, p_scr=p_scr, z_scr=z_scr,
        nb=nb, tl=tl, final=final)
    _run_tiles(x_ref, mod_ref, ng_ref, win_ref, proj_scr, pool, rest, tl=tl, tps=tps)

    @pl.when(step == last)
    def _fin():
        po_ref[...] = phist_scr[:, POOL_HIST - POOL_PAD:POOL_HIST, :]


def _odd_call(x, mods, pool_st, ng, w_in, gw, gb, sc, w_out, fg, *,
              layer, mod_row0, state_layer, nb, tl, tps, pos0):
    b, l, d = x.shape
    o = layer // 2
    final = layer == DEPTH - 1
    assert mod_row0 % nb == 0
    mod_blk0 = mod_row0 // nb
    m = nb * tl
    n_tiles = l // tl
    assert b % nb == 0 and l % (tl * tps) == 0 and tl % SEG == 0 and m % (SEG * SEGS) == 0
    kern = functools.partial(_odd_kernel, nb=nb, tl=tl, tps=tps, pos0=pos0, final=final)
    groups = m // (SEG * SEGS)
    return pl.pallas_call(
        kern,
        grid=(b // nb, n_tiles // tps),
        in_specs=[
            pl.BlockSpec((nb, tps * tl, d), lambda i, s: (i, s, 0)),
            pl.BlockSpec((None, 3, nb, 1, d), lambda i, s: (layer, 0, mod_blk0 + i, 0, 0)),
            pl.BlockSpec((None, nb, POOL_PAD, D_C), lambda i, s: (state_layer, i, 0, 0)),
            _layer_spec((1, d), layer),
            _layer_spec((d, 2 * D_C), o),
            _layer_spec((len(POOL_WINDOWS), POOL_GW, POOL_GW), o),
            _layer_spec((1, D_C), o),
            _layer_spec((1, D_C), o),
            _layer_spec((D_C, d), o),
            _const_spec((1, d)),
        ],
        out_specs=[
            pl.BlockSpec((nb, tps * tl, d), lambda i, s: (i, s, 0)),
            pl.BlockSpec((nb, POOL_PAD, D_C), lambda i, s: (i, 0, 0)),
        ],
        out_shape=[
            jax.ShapeDtypeStruct((b, l, d), F32),
            jax.ShapeDtypeStruct((b, POOL_PAD, D_C), F32),
        ],
        scratch_shapes=[
            pltpu.VMEM((min(tps, 2), 2 * D_C // MXU_COLS, m, MXU_COLS), F32),
            pltpu.VMEM((groups, P_SLAB, SEGS * POOL_PITCH, LANES), F32),
            pltpu.VMEM((groups, P_SLAB, SEGS * OUT_PITCH, LANES), F32),
            pltpu.VMEM((nb, POOL_HIST, D_C), F32),
            pltpu.VMEM((m, D_C), BF16),
            pltpu.VMEM((m, D_C), BF16),
        ],
        compiler_params=pltpu.CompilerParams(
            dimension_semantics=("arbitrary", "arbitrary"), vmem_limit_bytes=VMEM_LIMIT),
        name="odd_layer",
    )(x, mods, pool_st, ng, w_in, gw, gb, sc, w_out, fg)


PROMPT_TILE = dict(nb=4, tl=256, tps=1)
SAMPLE_TILE = dict(nb=8, tl=32, tps=1)


def kernel(x_prompt, x_sample, c_prompt, c_sample, state_conv, state_gla, state_pool, norm_g, ada_w, ada_b, ev_w_in, ev_conv_w, ev_conv_b, ev_ln_g, ev_ln_b, ev_gate_w2, ev_gate_b, ev_head_g, ev_w_out, od_w_in, od_group_w, od_group_b, od_scale, od_w_out, final_g):
    b_p = x_prompt.shape[0]
    b_s = x_sample.shape[0]
    n_even = ev_w_in.shape[0]
    n_odd = od_w_in.shape[0]

    c_all = jnp.concatenate([c_sample, c_prompt], axis=0)
    bp = -(-c_all.shape[0] // SUBLANES) * SUBLANES
    c_all = jnp.pad(c_all, ((0, bp - c_all.shape[0]), (0, 0)))
    mods = _ada_call(c_all, ada_w, ada_b).reshape(DEPTH, 3, bp, 1, D_MODEL)

    ev_w_in_b = jnp.pad(ev_w_in.astype(BF16), ((0, 0), (0, 0), (0, EV_IN_PAD - EV_IN)))
    ev_gw2_b = jnp.pad(ev_gate_w2.astype(BF16), ((0, 0), (0, LANES - GATE_RANK), (0, 0)))
    ev_w_out_b = ev_w_out.astype(BF16)
    od_w_in_b = od_w_in.astype(BF16)
    od_gw_b = od_group_w.astype(BF16)
    od_w_out_b = od_w_out.astype(BF16)

    zero_conv = jnp.zeros((1, b_p, CONV_PAD, D_A), F32)
    zero_gla = jnp.zeros((1, b_p, H_B, DK_B, DV_B), F32)
    zero_pool = jnp.zeros((1, b_p, POOL_PAD, D_C), F32)

    def rows(a):
        return a.reshape(a.shape[0], 1, a.shape[1])

    ng = rows(norm_g)
    ev_w = (ng, ev_w_in_b, ev_conv_w, rows(ev_conv_b), rows(ev_ln_g), rows(ev_ln_b), ev_gw2_b,
            rows(ev_gate_b), rows(ev_head_g), ev_w_out_b)
    od_w = (ng, od_w_in_b, od_gw_b, rows(od_group_b), rows(od_scale), od_w_out_b,
            final_g.reshape(1, D_MODEL))
    prompt = dict(mod_row0=b_s, state_layer=0, **PROMPT_TILE)

    xp, xs = x_prompt, x_sample
    conv_p, gla_p, pool_p, conv_s, gla_s, pool_s = [], [], [], [], [], []
    for l in range(DEPTH):
        sample = dict(mod_row0=0, state_layer=l // 2, **SAMPLE_TILE)
        if l % 2 == 0:
            xp, cp, gp = _even_call(xp, mods, zero_conv, zero_gla, *ev_w, layer=l, **prompt)
            xs, cs, gs = _even_call(xs, mods, state_conv, state_gla, *ev_w, layer=l, **sample)
            conv_p.append(cp)
            gla_p.append(gp)
            conv_s.append(cs)
            gla_s.append(gs)
        else:
            xp, pp = _odd_call(xp, mods, zero_pool, *od_w, layer=l, pos0=0, **prompt)
            xs, ps = _odd_call(xs, mods, state_pool, *od_w, layer=l, pos0=PAST_LEN, **sample)
            pool_p.append(pp)
            pool_s.append(ps)
    assert n_even == len(conv_p) and n_odd == len(pool_p)
    return (xp, xs, jnp.stack(conv_p), jnp.stack(gla_p), jnp.stack(pool_p),
            jnp.stack(conv_s), jnp.stack(gla_s), jnp.stack(pool_s))
```

```python
import functools

import numpy as np
import jax
import jax.numpy as jnp
from jax import lax
from jax.experimental import pallas as pl
from jax.experimental.pallas import tpu as pltpu

F32 = jnp.float32
BF16 = jnp.bfloat16

LANES = 128
SUBLANES = 8
VMEM_LIMIT = 56 * 1024 * 1024
MXU_COLS = 256
GROUP_ROWS = MXU_COLS

D_MODEL = 1024
DEPTH = 4
EPS = 1e-6
PAST_LEN = 2048
LOG2E = 1.4426950408889634

D_A = D_MODEL // 2
CONV_W = 31
CONV_PAD = CONV_W - 1
SEG = 32
SEGS = SUBLANES
SEG_PITCH = 68
OUT_PITCH = 36
N_SLAB = D_A // LANES
CONV_KB = 4

H_B = 4
DK_B = D_MODEL // 16
DV_B = D_MODEL // 8
D_BK = H_B * DK_B
D_BV = H_B * DV_B
GATE_RANK = 16
GATE_TAU = 16.0
GLA_CHUNK = 64
SUB_ROWS = LANES

D_C = D_MODEL
POOL_WINDOWS = (2, 4, 8, 16)
POOL_GW = D_C // len(POOL_WINDOWS)
POOL_PAD = 15
POOL_HIST = 16
POOL_PITCH = 52
P_SLAB = D_C // LANES

O_VAL, O_GLU, O_GATE = 0, D_A, 2 * D_A
O_Q = 3 * D_A
O_K = O_Q + D_BK
O_V = O_K + D_BK
O_BG = O_V + D_BV
O_LR = O_BG + D_BV
EV_IN_PAD = O_LR + LANES


def _sigmoid(x):
    return 1.0 / (1.0 + jnp.exp2(x * (-LOG2E)))


def _silu(x):
    return x * _sigmoid(x)


def _split_bf16(x):
    hi = x.astype(BF16)
    lo = (x - hi.astype(F32)).astype(BF16)
    return hi, lo


def _dot(a, b):
    return jnp.dot(a, b, preferred_element_type=F32)


def _const_spec(shape):
    nd = len(shape)
    return pl.BlockSpec(shape, lambda *_: (0,) * nd, pipeline_mode=pl.Buffered(1))


def _layer_spec(shape, layer):
    nd = len(shape)
    return pl.BlockSpec((None,) + tuple(shape), lambda *_: (layer,) + (0,) * nd,
                        pipeline_mode=pl.Buffered(1))


def _ada_kernel(c_ref, w_ref, b_ref, o_ref):
    cs = _silu(c_ref[...])
    acc = _dot(cs.astype(BF16), w_ref[0].astype(BF16))
    for j in range(3):
        o_ref[0, j] = acc[:, j * D_MODEL:(j + 1) * D_MODEL] + b_ref[0, j]


def _ada_call(c_all, ada_w, ada_b):
    bp = c_all.shape[0]
    return pl.pallas_call(
        _ada_kernel,
        grid=(DEPTH,),
        in_specs=[
            pl.BlockSpec((bp, D_MODEL), lambda l: (0, 0)),
            pl.BlockSpec((1, D_MODEL, 3 * D_MODEL), lambda l: (l, 0, 0)),
            pl.BlockSpec((1, 3, 1, D_MODEL), lambda l: (l, 0, 0, 0)),
        ],
        out_specs=pl.BlockSpec((1, 3, bp, D_MODEL), lambda l: (l, 0, 0, 0)),
        out_shape=jax.ShapeDtypeStruct((DEPTH, 3, bp, D_MODEL), F32),
        compiler_params=pltpu.CompilerParams(
            dimension_semantics=("arbitrary",), vmem_limit_bytes=VMEM_LIMIT),
        name="ada_mod",
    )(c_all, ada_w, ada_b.reshape(DEPTH, 3, 1, D_MODEL))


def _modulated_input(x3, mod_ref, ng_ref):
    nb, tl, d = x3.shape
    shift = mod_ref[0]
    scale = mod_ref[1]
    ms = jnp.mean(x3 * x3, axis=-1, keepdims=True)
    gs = ng_ref[...] * (1.0 + scale)
    h = (x3 * lax.rsqrt(ms + EPS)) * gs + shift
    return h.astype(BF16).reshape(nb * tl, d)


def _cols(proj, rows, c0, c1):
    parts = []
    while c0 < c1:
        p, off = divmod(c0, MXU_COLS)
        w = min(MXU_COLS - off, c1 - c0)
        parts.append(proj[p, rows, off:off + w])
        c0 += w
    return parts[0] if len(parts) == 1 else jnp.concatenate(parts, axis=1)


def _run_tiles(x_ref, mod_ref, ng_ref, win_refs, proj_scr, group_stage, rest_stage, *, tl, tps):
    pieces = [(w_ref, c0, min(c0 + MXU_COLS, w_ref.shape[1]))
              for w_ref in win_refs for c0 in range(0, w_ref.shape[1], MXU_COLS)]
    for j in range(tps):
        hb = _modulated_input(x_ref[:, j * tl:(j + 1) * tl, :], mod_ref, ng_ref)
        proj = proj_scr.at[j % 2]
        groups = hb.shape[0] // GROUP_ROWS
        for g in range(groups):
            rows = slice(g * GROUP_ROWS, (g + 1) * GROUP_ROWS)
            for p, (w_ref, c0, c1) in enumerate(pieces):
                proj[p, rows, 0:c1 - c0] = _dot(hb[rows], w_ref[:, c0:c1])
        for g in range(groups):
            group_stage(proj, j, g)
        rest_stage(proj, j)


def _even_conv(proj, j, g, cw_ref, cb_ref, lng_ref, lnb_ref, cbuf, obuf, hist_scr, y_scr,
               *, tl):
    del j
    segs_per_seq = tl // SEG
    prev = None
    for i in range(SEGS):
        gi = g * SEGS + i
        n, si = divmod(gi, segs_per_seq)
        rows = slice(gi * SEG, (gi + 1) * SEG)
        blk = _cols(proj, rows, O_VAL, O_VAL + D_A) * _sigmoid(_cols(proj, rows, O_GLU, O_GLU + D_A))
        halo = hist_scr[n] if prev is None or si == 0 else prev
        for s in range(N_SLAB):
            ls = slice(s * LANES, (s + 1) * LANES)
            cbuf[g, s,i * SEG_PITCH:i * SEG_PITCH + SEG, :] = halo[:, ls]
            cbuf[g, s,i * SEG_PITCH + SEG:i * SEG_PITCH + 2 * SEG, :] = blk[:, ls]
        if i == SEGS - 1 or si == segs_per_seq - 1:
            hist_scr[n] = blk
        prev = blk

    inv_da = 1.0 / D_A
    for kb in range(SEG // CONV_KB):
        acc = [[jnp.broadcast_to(cb_ref[:, s * LANES:(s + 1) * LANES], (SEGS, LANES))
                for s in range(N_SLAB)] for _ in range(CONV_KB)]
        for jt in range(CONV_W):
            for s in range(N_SLAB):
                wj = cw_ref[jt:jt + 1, s * LANES:(s + 1) * LANES]
                for kk in range(CONV_KB):
                    start = SEG - CONV_PAD + kb * CONV_KB + kk + jt
                    acc[kk][s] = acc[kk][s] + wj * cbuf[g, s, pl.ds(start, SEGS, stride=SEG_PITCH), :]
        for kk in range(CONV_KB):
            a = acc[kk]
            mu = jnp.sum(a[0] + a[1] + a[2] + a[3], axis=-1, keepdims=True) * inv_da
            xc = [v - mu for v in a]
            sq = xc[0] * xc[0] + xc[1] * xc[1] + xc[2] * xc[2] + xc[3] * xc[3]
            rstd = lax.rsqrt(jnp.sum(sq, axis=-1, keepdims=True) * inv_da + EPS)
            for s in range(N_SLAB):
                ls = slice(s * LANES, (s + 1) * LANES)
                yn = xc[s] * rstd * lng_ref[:, ls] + lnb_ref[:, ls]
                obuf[g, s, pl.ds(kb * CONV_KB + kk, SEGS, stride=OUT_PITCH), :] = _silu(yn)

    for i in range(SEGS):
        gi = g * SEGS + i
        rows = slice(gi * SEG, (gi + 1) * SEG)
        for s in range(N_SLAB):
            ya = obuf[g, s,i * OUT_PITCH:i * OUT_PITCH + SEG, :] * _silu(
                _cols(proj, rows, O_GATE + s * LANES, O_GATE + (s + 1) * LANES))
            y_scr[rows, s * LANES:(s + 1) * LANES] = ya.astype(BF16)


def _even_rest(proj, j, x_ref, mod_ref, gw2_ref, gb_ref, hg_ref, wout_ref, tri_ref, ones_ref,
               xo_ref, st_scr, o_scr, y_scr, *, nb, tl, chunk):
    m = nb * tl
    gr = min(m, GROUP_ROWS)
    lane = lax.broadcasted_iota(jnp.int32, (1, LANES), 1)
    lo_half = lane < DK_B

    every = slice(None)
    pre = _dot(_cols(proj, every, O_LR, O_LR + LANES).astype(BF16), gw2_ref[...]) + gb_ref[...]
    la = (jnp.minimum(pre, 0.0) - jnp.log1p(jnp.exp(-jnp.abs(pre)))) * (1.0 / GATE_TAU)
    la_hi, la_lo = _split_bf16(la)
    tri = tri_ref[...]
    ones_bd = ones_ref[...]
    causal = tri_ref[0:SUB_ROWS, 0:SUB_ROWS].astype(F32) > 0.5
    pair_lo = jnp.concatenate([lo_half, lo_half], axis=1)
    srow = lax.broadcasted_iota(jnp.int32, (LANES, 2 * DV_B), 0)
    scol = lax.broadcasted_iota(jnp.int32, (LANES, 2 * DV_B), 1)
    on_diag = (srow < DK_B) == (scol < DV_B)
    in_chunk = [(lane >= c * chunk) & (lane < (c + 1) * chunk) for c in range(SUB_ROWS // chunk)]
    for g in range(m // gr):
        rows = slice(g * gr, (g + 1) * gr)
        b = _dot(tri, la_hi[rows]) + _dot(tri, la_lo[rows])
        bl = _dot(ones_bd, la_hi[rows]) + _dot(ones_bd, la_lo[rows])
        k = _cols(proj, rows, O_K, O_K + D_BK)
        q_e = _cols(proj, rows, O_Q, O_Q + D_BK) * (DK_B ** -0.5) * jnp.exp(b)
        q_pair = q_e.astype(BF16)
        q_even = jnp.where(pair_lo, q_e, 0.0).astype(BF16)
        q_odd = jnp.where(pair_lo, 0.0, q_e).astype(BF16)
        k_e = k * jnp.exp(-b)
        k_l = k * jnp.exp(bl - b)
        dec = jnp.exp(bl)
        vb = _cols(proj, rows, O_V, O_V + D_BV).astype(BF16)
        for sg in range(gr // SUB_ROWS):
            sr = slice(sg * SUB_ROWS, (sg + 1) * SUB_ROWS)
            for p in range(H_B // 2):
                pc = slice(p * LANES, (p + 1) * LANES)
                vc = slice(p * 2 * DV_B, (p + 1) * 2 * DV_B)
                k_et = k_e[sr, pc].T.astype(BF16)
                k_lt = k_l[sr, pc].T
                v2 = vb[sr, vc]
                att = _dot(jnp.concatenate([q_even[sr, pc], q_odd[sr, pc]], axis=0), k_et)
                att_a = jnp.where(causal, att[0:SUB_ROWS], 0.0).astype(BF16)
                att_b = jnp.where(causal, att[SUB_ROWS:2 * SUB_ROWS], 0.0).astype(BF16)
                o_intra = jnp.concatenate(
                    [_dot(att_a, v2[:, 0:DV_B]), _dot(att_b, v2[:, DV_B:2 * DV_B])], axis=1)
                for c in range(SUB_ROWS // chunk):
                    c0 = sg * SUB_ROWS + c * chunk
                    row0 = g * gr + c0
                    n = row0 // tl
                    s_bd = st_scr[n, p]
                    o_inter = _dot(q_pair[c0:c0 + chunk, pc], s_bd.astype(BF16))
                    o_scr[row0:row0 + chunk, vc] = o_intra[c * chunk:(c + 1) * chunk] + o_inter
                    upd = _dot(jnp.where(in_chunk[c], k_lt, 0.0).astype(BF16), v2)
                    dcol = jnp.broadcast_to(dec[c0:c0 + 1, pc], (LANES, LANES)).T
                    st_scr[n, p] = (jnp.concatenate([dcol, dcol], axis=1) * s_bd
                                    + jnp.where(on_diag, upd, 0.0))

    for hh in range(H_B):
        vc = slice(hh * DV_B, (hh + 1) * DV_B)
        oh = o_scr[:, vc]
        ms = jnp.mean(oh * oh, axis=-1, keepdims=True)
        on = oh * lax.rsqrt(ms + EPS) * hg_ref[:, vc]
        yb = on * _silu(_cols(proj, every, O_BG + hh * DV_B, O_BG + (hh + 1) * DV_B))
        y_scr[:, D_A + hh * DV_B:D_A + (hh + 1) * DV_B] = yb.astype(BF16)

    y = _dot(y_scr[...], wout_ref[...])
    tr = slice(j * tl, (j + 1) * tl)
    xo_ref[:, tr, :] = x_ref[:, tr, :] + mod_ref[2] * y.reshape(nb, tl, D_MODEL)


def _even_kernel(x_ref, mod_ref, cst_ref, gst_ref, ng_ref, win_ref, wlr_ref, cw_ref, cb_ref, lng_ref,
                 lnb_ref, gw2_ref, gb_ref, hg_ref, wout_ref, tri_ref, ones_ref,
                 xo_ref, co_ref, go_ref,
                 proj_scr, cbuf, obuf, hist_scr, st_scr, o_scr, y_scr, *, nb, tl, tps, chunk):
    step = pl.program_id(1)
    last = pl.num_programs(1) - 1

    @pl.when(step == 0)
    def _init():
        hist_scr[...] = jnp.zeros_like(hist_scr)
        hist_scr[:, SEG - CONV_PAD:SEG, :] = cst_ref[...]
        z = jnp.zeros((DK_B, DV_B), F32)
        for n in range(nb):
            for p in range(H_B // 2):
                top = jnp.concatenate([gst_ref[n, 2 * p], z], axis=1)
                bot = jnp.concatenate([z, gst_ref[n, 2 * p + 1]], axis=1)
                st_scr[n, p] = jnp.concatenate([top, bot], axis=0)

    conv = functools.partial(
        _even_conv, cw_ref=cw_ref, cb_ref=cb_ref, lng_ref=lng_ref, lnb_ref=lnb_ref, cbuf=cbuf,
        obuf=obuf, hist_scr=hist_scr, y_scr=y_scr, tl=tl)
    rest = functools.partial(
        _even_rest, x_ref=x_ref, mod_ref=mod_ref, gw2_ref=gw2_ref, gb_ref=gb_ref, hg_ref=hg_ref,
        wout_ref=wout_ref, tri_ref=tri_ref, ones_ref=ones_ref, xo_ref=xo_ref, st_scr=st_scr,
        o_scr=o_scr, y_scr=y_scr, nb=nb, tl=tl, chunk=chunk)
    _run_tiles(x_ref, mod_ref, ng_ref, (win_ref, wlr_ref), proj_scr, conv, rest, tl=tl, tps=tps)

    @pl.when(step == last)
    def _fin():
        co_ref[...] = hist_scr[:, SEG - CONV_PAD:SEG, :]
        for n in range(nb):
            for p in range(H_B // 2):
                go_ref[n, 2 * p] = st_scr[n, p, 0:DK_B, 0:DV_B]
                go_ref[n, 2 * p + 1] = st_scr[n, p, DK_B:2 * DK_B, DV_B:2 * DV_B]


def _decay_matrices(rows, chunk):
    i = np.arange(rows)
    same = (i[:, None] // chunk) == (i[None, :] // chunk)
    tri = same & (i[:, None] >= i[None, :])
    return jnp.asarray(tri, BF16), jnp.asarray(same, BF16)


def _even_call(x, mods, conv_st, gla_st, ng, w_in, w_lr, cw, cb, lng, lnb, gw2, gb, hg, w_out, *,
               layer, mod_row0, state_layer, nb, tl, tps):
    b, l, d = x.shape
    e = layer // 2
    assert mod_row0 % nb == 0
    mod_blk0 = mod_row0 // nb
    chunk = min(GLA_CHUNK, l)
    m = nb * tl
    gr = min(m, GROUP_ROWS)
    n_tiles = l // tl
    assert b % nb == 0 and l % (tl * tps) == 0 and tl % chunk == 0 and m % gr == 0 and gr % chunk == 0
    assert tl % SEG == 0 and m % (SEG * SEGS) == 0
    assert gr % SUB_ROWS == 0 and SUB_ROWS % chunk == 0 and 2 * DK_B == LANES
    tri, ones_bd = _decay_matrices(gr, chunk)
    kern = functools.partial(_even_kernel, nb=nb, tl=tl, tps=tps, chunk=chunk)
    groups = m // (SEG * SEGS)
    return pl.pallas_call(
        kern,
        grid=(b // nb, n_tiles // tps),
        in_specs=[
            pl.BlockSpec((nb, tps * tl, d), lambda i, s: (i, s, 0)),
            pl.BlockSpec((None, 3, nb, 1, d), lambda i, s: (layer, 0, mod_blk0 + i, 0, 0)),
            pl.BlockSpec((None, nb, CONV_PAD, D_A), lambda i, s: (state_layer, i, 0, 0)),
            pl.BlockSpec((None, nb, H_B, DK_B, DV_B), lambda i, s: (state_layer, i, 0, 0, 0)),
            _layer_spec((1, d), layer),
            _layer_spec((d, O_LR), e),
            _layer_spec((d, LANES), e),
            _layer_spec((CONV_W, D_A), e),
            _layer_spec((1, D_A), e),
            _layer_spec((1, D_A), e),
            _layer_spec((1, D_A), e),
            _layer_spec((LANES, D_BK), e),
            _layer_spec((1, D_BK), e),
            _layer_spec((1, D_BV), e),
            _layer_spec((D_A + D_BV, d), e),
            _const_spec((gr, gr)),
            _const_spec((gr, gr)),
        ],
        out_specs=[
            pl.BlockSpec((nb, tps * tl, d), lambda i, s: (i, s, 0)),
            pl.BlockSpec((nb, CONV_PAD, D_A), lambda i, s: (i, 0, 0)),
            pl.BlockSpec((nb, H_B, DK_B, DV_B), lambda i, s: (i, 0, 0, 0)),
        ],
        out_shape=[
            jax.ShapeDtypeStruct((b, l, d), F32),
            jax.ShapeDtypeStruct((b, CONV_PAD, D_A), F32),
            jax.ShapeDtypeStruct((b, H_B, DK_B, DV_B), F32),
        ],
        scratch_shapes=[
            pltpu.VMEM((min(tps, 2), -(-EV_IN_PAD // MXU_COLS), m, MXU_COLS), F32),
            pltpu.VMEM((groups, N_SLAB, SEGS * SEG_PITCH, LANES), F32),
            pltpu.VMEM((groups, N_SLAB, SEGS * OUT_PITCH, LANES), F32),
            pltpu.VMEM((nb, SEG, D_A), F32),
            pltpu.VMEM((nb, H_B // 2, LANES, 2 * DV_B), F32),
            pltpu.VMEM((m, D_BV), F32),
            pltpu.VMEM((m, D_A + D_BV), BF16),
        ],
        compiler_params=pltpu.CompilerParams(
            dimension_semantics=("arbitrary", "arbitrary"), vmem_limit_bytes=VMEM_LIMIT),
        name="even_layer",
    )(x, mods, conv_st, gla_st, ng, w_in, w_lr, cw, cb, lng, lnb, gw2, gb, hg, w_out, tri, ones_bd)


def _odd_pool(proj, j, g, vbuf, pbuf, phist_scr, p_scr, *, tl, tps, pos0):
    tile_pos = pos0 + (pl.program_id(1) * tps + j) * tl
    segs_per_seq = tl // SEG
    prev = None
    for i in range(SEGS):
        gi = g * SEGS + i
        n, si = divmod(gi, segs_per_seq)
        blk = _cols(proj, slice(gi * SEG, (gi + 1) * SEG), 0, D_C)
        halo = phist_scr[n] if prev is None or si == 0 else prev[SEG - POOL_HIST:SEG, :]
        for s in range(P_SLAB):
            ls = slice(s * LANES, (s + 1) * LANES)
            vbuf[g, s,i * POOL_PITCH:i * POOL_PITCH + POOL_HIST, :] = halo[:, ls]
            vbuf[g, s,i * POOL_PITCH + POOL_HIST:i * POOL_PITCH + POOL_HIST + SEG, :] = blk[:, ls]
        if i == SEGS - 1 or si == segs_per_seq - 1:
            phist_scr[n] = blk[SEG - POOL_HIST:SEG, :]
        prev = blk

    sub = lax.broadcasted_iota(jnp.int32, (SEGS, LANES), 0)
    seg_pos1 = tile_pos + 1 + ((sub + g * SEGS) % segs_per_seq) * SEG
    for wi, w in enumerate(POOL_WINDOWS):
        for s in range(wi * (POOL_GW // LANES), (wi + 1) * (POOL_GW // LANES)):
            lo = -(w - 1)
            cur = {k: vbuf[g, s,pl.ds(POOL_HIST + k, SEGS, stride=POOL_PITCH), :]
                   for k in range(lo, SEG)}
            xk = cur
            d = 1
            while d < w:
                lo += d
                cur = {k: cur[k] + cur[k - d] for k in range(lo, SEG)}
                d *= 2
            for k in range(SEG):
                if k >= w - 1:
                    p = cur[k] * (1.0 / w) - xk[k]
                else:
                    cnt = jnp.minimum(seg_pos1 + k, w).astype(F32)
                    p = cur[k] / cnt - xk[k]
                pbuf[g, s, pl.ds(k, SEGS, stride=OUT_PITCH), :] = p

    for i in range(SEGS):
        gi = g * SEGS + i
        rows = slice(gi * SEG, (gi + 1) * SEG)
        for s in range(P_SLAB):
            p_scr[rows, s * LANES:(s + 1) * LANES] = pbuf[
                g, s, i * OUT_PITCH:i * OUT_PITCH + SEG, :].astype(BF16)


def _odd_rest(proj, j, x_ref, mod_ref, gw_ref, gb_ref, sc_ref, wout_ref, fg_ref, xo_ref,
              p_scr, z_scr, *, nb, tl, final):
    for gi in range(len(POOL_WINDOWS)):
        cs = slice(gi * POOL_GW, (gi + 1) * POOL_GW)
        pg = _dot(p_scr[:, cs], gw_ref[gi]) + gb_ref[:, cs]
        z = pg * sc_ref[:, cs] * _silu(
            _cols(proj, slice(None), D_C + gi * POOL_GW, D_C + (gi + 1) * POOL_GW))
        z_scr[:, cs] = z.astype(BF16)

    y = _dot(z_scr[...], wout_ref[...])
    tr = slice(j * tl, (j + 1) * tl)
    xo = x_ref[:, tr, :] + mod_ref[2] * y.reshape(nb, tl, D_MODEL)
    if final:
        ms = jnp.mean(xo * xo, axis=-1, keepdims=True)
        xo = xo * lax.rsqrt(ms + EPS) * fg_ref[...]
    xo_ref[:, tr, :] = xo


def _odd_kernel(x_ref, mod_ref, pst_ref, ng_ref, win_ref, gw_ref, gb_ref, sc_ref, wout_ref, fg_ref,
                xo_ref, po_ref,
                proj_scr, vbuf, pbuf, phist_scr, p_scr, z_scr, *, nb, tl, tps, pos0, final):
    step = pl.program_id(1)
    last = pl.num_programs(1) - 1

    @pl.when(step == 0)
    def _init():
        phist_scr[...] = jnp.zeros_like(phist_scr)
        phist_scr[:, POOL_HIST - POOL_PAD:POOL_HIST, :] = pst_ref[...]

    pool = functools.partial(
        _odd_pool, vbuf=vbuf, pbuf=pbuf, phist_scr=phist_scr, p_scr=p_scr, tl=tl, tps=tps, pos0=pos0)
    rest = functools.partial(
        _odd_rest, x_ref=x_ref, mod_ref=mod_ref, gw_ref=gw_ref, gb_ref=gb_ref, sc_ref=sc_ref,
        wout_ref=wout_ref, fg_ref=fg_ref, xo_ref=xo_ref, p_scr=p_scr, z_scr=z_scr,
        nb=nb, tl=tl, final=final)
    _run_tiles(x_ref, mod_ref, ng_ref, (win_ref,), proj_scr, pool, rest, tl=tl, tps=tps)

    @pl.when(step == last)
    def _fin():
        po_ref[...] = phist_scr[:, POOL_HIST - POOL_PAD:POOL_HIST, :]


def _odd_call(x, mods, pool_st, ng, w_in, gw, gb, sc, w_out, fg, *,
              layer, mod_row0, state_layer, nb, tl, tps, pos0):
    b, l, d = x.shape
    o = layer // 2
    final = layer == DEPTH - 1
    assert mod_row0 % nb == 0
    mod_blk0 = mod_row0 // nb
    m = nb * tl
    n_tiles = l // tl
    assert b % nb == 0 and l % (tl * tps) == 0 and tl % SEG == 0 and m % (SEG * SEGS) == 0
    kern = functools.partial(_odd_kernel, nb=nb, tl=tl, tps=tps, pos0=pos0, final=final)
    groups = m // (SEG * SEGS)
    return pl.pallas_call(
        kern,
        grid=(b // nb, n_tiles // tps),
        in_specs=[
            pl.BlockSpec((nb, tps * tl, d), lambda i, s: (i, s, 0)),
            pl.BlockSpec((None, 3, nb, 1, d), lambda i, s: (layer, 0, mod_blk0 + i, 0, 0)),
            pl.BlockSpec((None, nb, POOL_PAD, D_C), lambda i, s: (state_layer, i, 0, 0)),
            _layer_spec((1, d), layer),
            _layer_spec((d, 2 * D_C), o),
            _layer_spec((len(POOL_WINDOWS), POOL_GW, POOL_GW), o),
            _layer_spec((1, D_C), o),
            _layer_spec((1, D_C), o),
            _layer_spec((D_C, d), o),
            _const_spec((1, d)),
        ],
        out_specs=[
            pl.BlockSpec((nb, tps * tl, d), lambda i, s: (i, s, 0)),
            pl.BlockSpec((nb, POOL_PAD, D_C), lambda i, s: (i, 0, 0)),
        ],
        out_shape=[
            jax.ShapeDtypeStruct((b, l, d), F32),
            jax.ShapeDtypeStruct((b, POOL_PAD, D_C), F32),
        ],
        scratch_shapes=[
            pltpu.VMEM((min(tps, 2), 2 * D_C // MXU_COLS, m, MXU_COLS), F32),
            pltpu.VMEM((groups, P_SLAB, SEGS * POOL_PITCH, LANES), F32),
            pltpu.VMEM((groups, P_SLAB, SEGS * OUT_PITCH, LANES), F32),
            pltpu.VMEM((nb, POOL_HIST, D_C), F32),
            pltpu.VMEM((m, D_C), BF16),
            pltpu.VMEM((m, D_C), BF16),
        ],
        compiler_params=pltpu.CompilerParams(
            dimension_semantics=("arbitrary", "arbitrary"), vmem_limit_bytes=VMEM_LIMIT),
        name="odd_layer",
    )(x, mods, pool_st, ng, w_in, gw, gb, sc, w_out, fg)


PROMPT_TILE = dict(nb=4, tl=256, tps=1)
SAMPLE_TILE = dict(nb=8, tl=32, tps=1)


def kernel(x_prompt, x_sample, c_prompt, c_sample, state_conv, state_gla, state_pool, norm_g, ada_w, ada_b, ev_w_in, ev_conv_w, ev_conv_b, ev_ln_g, ev_ln_b, ev_gate_w2, ev_gate_b, ev_head_g, ev_w_out, od_w_in, od_group_w, od_group_b, od_scale, od_w_out, final_g):
    b_p = x_prompt.shape[0]
    b_s = x_sample.shape[0]
    n_even = ev_w_in.shape[0]
    n_odd = od_w_in.shape[0]

    c_all = jnp.concatenate([c_sample, c_prompt], axis=0)
    bp = -(-c_all.shape[0] // SUBLANES) * SUBLANES
    c_all = jnp.pad(c_all, ((0, bp - c_all.shape[0]), (0, 0)))
    mods = _ada_call(c_all, ada_w, ada_b).reshape(DEPTH, 3, bp, 1, D_MODEL)

    ev_w_in_b = ev_w_in[:, :, :O_LR].astype(BF16)
    ev_w_lr_b = jnp.pad(ev_w_in[:, :, O_LR:].astype(BF16), ((0, 0), (0, 0), (0, LANES - GATE_RANK)))
    ev_gw2_b = jnp.pad(ev_gate_w2.astype(BF16), ((0, 0), (0, LANES - GATE_RANK), (0, 0)))
    ev_w_out_b = ev_w_out.astype(BF16)
    od_w_in_b = od_w_in.astype(BF16)
    od_gw_b = od_group_w.astype(BF16)
    od_w_out_b = od_w_out.astype(BF16)

    zero_conv = jnp.zeros((1, b_p, CONV_PAD, D_A), F32)
    zero_gla = jnp.zeros((1, b_p, H_B, DK_B, DV_B), F32)
    zero_pool = jnp.zeros((1, b_p, POOL_PAD, D_C), F32)

    def rows(a):
        return a.reshape(a.shape[0], 1, a.shape[1])

    ng = rows(norm_g)
    ev_w = (ng, ev_w_in_b, ev_w_lr_b, ev_conv_w, rows(ev_conv_b), rows(ev_ln_g), rows(ev_ln_b), ev_gw2_b,
            rows(ev_gate_b), rows(ev_head_g), ev_w_out_b)
    od_w = (ng, od_w_in_b, od_gw_b, rows(od_group_b), rows(od_scale), od_w_out_b,
            final_g.reshape(1, D_MODEL))
    prompt = dict(mod_row0=b_s, state_layer=0, **PROMPT_TILE)

    xp, xs = x_prompt, x_sample
    conv_p, gla_p, pool_p, conv_s, gla_s, pool_s = [], [], [], [], [], []
    for l in range(DEPTH):
        sample = dict(mod_row0=0, state_layer=l // 2, **SAMPLE_TILE)
        if l % 2 == 0:
            xp, cp, gp = _even_call(xp, mods, zero_conv, zero_gla, *ev_w, layer=l, **prompt)
            xs, cs, gs = _even_call(xs, mods, state_conv, state_gla, *ev_w, layer=l, **sample)
            conv_p.append(cp)
            gla_p.append(gp)
            conv_s.append(cs)
            gla_s.append(gs)
        else:
            xp, pp = _odd_call(xp, mods, zero_pool, *od_w, layer=l, pos0=0, **prompt)
            xs, ps = _odd_call(xs, mods, state_pool, *od_w, layer=l, pos0=PAST_LEN, **sample)
            pool_p.append(pp)
            pool_s.append(ps)
    assert n_even == len(conv_p) and n_odd == len(pool_p)
    return (xp, xs, jnp.stack(conv_p), jnp.stack(gla_p), jnp.stack(pool_p),
            jnp.stack(conv_s), jnp.stack(gla_s), jnp.stack(pool_s))
```

```python
import functools

import numpy as np
import jax
import jax.numpy as jnp
from jax import lax
from jax.experimental import pallas as pl
from jax.experimental.pallas import tpu as pltpu

F32 = jnp.float32
BF16 = jnp.bfloat16

LANES = 128
SUBLANES = 8
VMEM_LIMIT = 56 * 1024 * 1024
MXU_COLS = 256
GROUP_ROWS = MXU_COLS

D_MODEL = 1024
DEPTH = 4
EPS = 1e-6
PAST_LEN = 2048
LOG2E = 1.4426950408889634

D_A = D_MODEL // 2
CONV_W = 31
CONV_PAD = CONV_W - 1
SEG = 32
SEGS = SUBLANES
SEG_PITCH = 68
OUT_PITCH = 36
N_SLAB = D_A // LANES
CONV_KB = 4

H_B = 4
DK_B = D_MODEL // 16
DV_B = D_MODEL // 8
D_BK = H_B * DK_B
D_BV = H_B * DV_B
GATE_RANK = 16
GATE_TAU = 16.0
GLA_CHUNK = 64
SUB_ROWS = LANES

D_C = D_MODEL
POOL_WINDOWS = (2, 4, 8, 16)
POOL_GW = D_C // len(POOL_WINDOWS)
POOL_PAD = 15
POOL_HIST = 16
POOL_PITCH = 52
P_SLAB = D_C // LANES

O_VAL, O_GLU, O_GATE = 0, D_A, 2 * D_A
O_Q = 3 * D_A
O_K = O_Q + D_BK
O_V = O_K + D_BK
O_BG = O_V + D_BV
O_LR = O_BG + D_BV
EV_IN = O_LR + GATE_RANK
EV_IN_PAD = O_LR + LANES


def _sigmoid(x):
    return 1.0 / (1.0 + jnp.exp2(x * (-LOG2E)))


def _silu(x):
    return x * _sigmoid(x)


def _split_bf16(x):
    hi = x.astype(BF16)
    lo = (x - hi.astype(F32)).astype(BF16)
    return hi, lo


def _dot(a, b):
    return jnp.dot(a, b, preferred_element_type=F32)


def _const_spec(shape):
    nd = len(shape)
    return pl.BlockSpec(shape, lambda *_: (0,) * nd, pipeline_mode=pl.Buffered(1))


def _layer_spec(shape, layer):
    nd = len(shape)
    return pl.BlockSpec((None,) + tuple(shape), lambda *_: (layer,) + (0,) * nd,
                        pipeline_mode=pl.Buffered(1))


def _ada_kernel(c_ref, w_ref, b_ref, o_ref):
    cs = _silu(c_ref[...])
    acc = _dot(cs.astype(BF16), w_ref[0].astype(BF16))
    for j in range(3):
        o_ref[0, j] = acc[:, j * D_MODEL:(j + 1) * D_MODEL] + b_ref[0, j]


def _ada_call(c_all, ada_w, ada_b):
    bp = c_all.shape[0]
    return pl.pallas_call(
        _ada_kernel,
        grid=(DEPTH,),
        in_specs=[
            pl.BlockSpec((bp, D_MODEL), lambda l: (0, 0)),
            pl.BlockSpec((1, D_MODEL, 3 * D_MODEL), lambda l: (l, 0, 0)),
            pl.BlockSpec((1, 3, 1, D_MODEL), lambda l: (l, 0, 0, 0)),
        ],
        out_specs=pl.BlockSpec((1, 3, bp, D_MODEL), lambda l: (l, 0, 0, 0)),
        out_shape=jax.ShapeDtypeStruct((DEPTH, 3, bp, D_MODEL), F32),
        compiler_params=pltpu.CompilerParams(
            dimension_semantics=("arbitrary",), vmem_limit_bytes=VMEM_LIMIT),
        name="ada_mod",
    )(c_all, ada_w, ada_b.reshape(DEPTH, 3, 1, D_MODEL))


def _modulated_input(x3, mod_ref, ng_ref):
    nb, tl, d = x3.shape
    shift = mod_ref[0]
    scale = mod_ref[1]
    ms = jnp.mean(x3 * x3, axis=-1, keepdims=True)
    gs = ng_ref[...] * (1.0 + scale)
    h = (x3 * lax.rsqrt(ms + EPS)) * gs + shift
    return h.astype(BF16).reshape(nb * tl, d)


def _cols(proj, rows, c0, c1):
    parts = []
    while c0 < c1:
        p, off = divmod(c0, MXU_COLS)
        w = min(MXU_COLS - off, c1 - c0)
        parts.append(proj[p, rows, off:off + w])
        c0 += w
    return parts[0] if len(parts) == 1 else jnp.concatenate(parts, axis=1)


def _run_tiles(x_ref, mod_ref, ng_ref, win_ref, proj_scr, group_stage, rest_stage, *, tl, tps):
    width = win_ref.shape[1]
    for j in range(tps):
        hb = _modulated_input(x_ref[:, j * tl:(j + 1) * tl, :], mod_ref, ng_ref)
        proj = proj_scr.at[j % 2]
        groups = hb.shape[0] // GROUP_ROWS
        for g in range(groups):
            rows = slice(g * GROUP_ROWS, (g + 1) * GROUP_ROWS)
            for p, c0 in enumerate(range(0, width, MXU_COLS)):
                c1 = min(c0 + MXU_COLS, width)
                proj[p, rows, 0:c1 - c0] = _dot(hb[rows], win_ref[:, c0:c1])
        for g in range(groups):
            group_stage(proj, j, g)
        rest_stage(proj, j)


def _even_conv(proj, j, g, cw_ref, cb_ref, lng_ref, lnb_ref, cbuf, obuf, hist_scr, y_scr,
               *, tl):
    del j
    segs_per_seq = tl // SEG
    prev = None
    for i in range(SEGS):
        gi = g * SEGS + i
        n, si = divmod(gi, segs_per_seq)
        rows = slice(gi * SEG, (gi + 1) * SEG)
        blk = _cols(proj, rows, O_VAL, O_VAL + D_A) * _sigmoid(_cols(proj, rows, O_GLU, O_GLU + D_A))
        halo = hist_scr[n] if prev is None or si == 0 else prev
        for s in range(N_SLAB):
            ls = slice(s * LANES, (s + 1) * LANES)
            cbuf[g, s,i * SEG_PITCH:i * SEG_PITCH + SEG, :] = halo[:, ls]
            cbuf[g, s,i * SEG_PITCH + SEG:i * SEG_PITCH + 2 * SEG, :] = blk[:, ls]
        if i == SEGS - 1 or si == segs_per_seq - 1:
            hist_scr[n] = blk
        prev = blk

    inv_da = 1.0 / D_A
    for kb in range(SEG // CONV_KB):
        acc = [[jnp.broadcast_to(cb_ref[:, s * LANES:(s + 1) * LANES], (SEGS, LANES))
                for s in range(N_SLAB)] for _ in range(CONV_KB)]
        for jt in range(CONV_W):
            for s in range(N_SLAB):
                wj = cw_ref[jt:jt + 1, s * LANES:(s + 1) * LANES]
                for kk in range(CONV_KB):
                    start = SEG - CONV_PAD + kb * CONV_KB + kk + jt
                    acc[kk][s] = acc[kk][s] + wj * cbuf[g, s, pl.ds(start, SEGS, stride=SEG_PITCH), :]
        for kk in range(CONV_KB):
            a = acc[kk]
            mu = jnp.sum(a[0] + a[1] + a[2] + a[3], axis=-1, keepdims=True) * inv_da
            xc = [v - mu for v in a]
            sq = xc[0] * xc[0] + xc[1] * xc[1] + xc[2] * xc[2] + xc[3] * xc[3]
            rstd = lax.rsqrt(jnp.sum(sq, axis=-1, keepdims=True) * inv_da + EPS)
            for s in range(N_SLAB):
                ls = slice(s * LANES, (s + 1) * LANES)
                yn = xc[s] * rstd * lng_ref[:, ls] + lnb_ref[:, ls]
                obuf[g, s, pl.ds(kb * CONV_KB + kk, SEGS, stride=OUT_PITCH), :] = _silu(yn)

    for i in range(SEGS):
        gi = g * SEGS + i
        rows = slice(gi * SEG, (gi + 1) * SEG)
        for s in range(N_SLAB):
            ya = obuf[g, s,i * OUT_PITCH:i * OUT_PITCH + SEG, :] * _silu(
                _cols(proj, rows, O_GATE + s * LANES, O_GATE + (s + 1) * LANES))
            y_scr[rows, s * LANES:(s + 1) * LANES] = ya.astype(BF16)


def _even_rest(proj, j, x_ref, mod_ref, gw2_ref, gb_ref, hg_ref, wout_ref, tri_ref, ones_ref,
               xo_ref, st_scr, o_scr, y_scr, *, nb, tl, chunk):
    m = nb * tl
    gr = min(m, GROUP_ROWS)
    lane = lax.broadcasted_iota(jnp.int32, (1, LANES), 1)
    lo_half = lane < DK_B

    every = slice(None)
    pre = _dot(_cols(proj, every, O_LR, O_LR + LANES).astype(BF16), gw2_ref[...]) + gb_ref[...]
    la = (jnp.minimum(pre, 0.0) - jnp.log1p(jnp.exp(-jnp.abs(pre)))) * (1.0 / GATE_TAU)
    la_hi, la_lo = _split_bf16(la)
    tri = tri_ref[...]
    ones_bd = ones_ref[...]
    causal = tri_ref[0:SUB_ROWS, 0:SUB_ROWS].astype(F32) > 0.5
    pair_lo = jnp.concatenate([lo_half, lo_half], axis=1)
    srow = lax.broadcasted_iota(jnp.int32, (LANES, 2 * DV_B), 0)
    scol = lax.broadcasted_iota(jnp.int32, (LANES, 2 * DV_B), 1)
    on_diag = (srow < DK_B) == (scol < DV_B)
    in_chunk = [(lane >= c * chunk) & (lane < (c + 1) * chunk) for c in range(SUB_ROWS // chunk)]
    for g in range(m // gr):
        rows = slice(g * gr, (g + 1) * gr)
        b = _dot(tri, la_hi[rows]) + _dot(tri, la_lo[rows])
        bl = _dot(ones_bd, la_hi[rows]) + _dot(ones_bd, la_lo[rows])
        k = _cols(proj, rows, O_K, O_K + D_BK)
        q_e = _cols(proj, rows, O_Q, O_Q + D_BK) * (DK_B ** -0.5) * jnp.exp(b)
        q_pair = q_e.astype(BF16)
        q_even = jnp.where(pair_lo, q_e, 0.0).astype(BF16)
        q_odd = jnp.where(pair_lo, 0.0, q_e).astype(BF16)
        k_e = k * jnp.exp(-b)
        k_l = k * jnp.exp(bl - b)
        dec = jnp.exp(bl)
        vb = _cols(proj, rows, O_V, O_V + D_BV).astype(BF16)
        for sg in range(gr // SUB_ROWS):
            sr = slice(sg * SUB_ROWS, (sg + 1) * SUB_ROWS)
            for p in range(H_B // 2):
                pc = slice(p * LANES, (p + 1) * LANES)
                vc = slice(p * 2 * DV_B, (p + 1) * 2 * DV_B)
                k_et = k_e[sr, pc].T.astype(BF16)
                k_lt = k_l[sr, pc].T
                v2 = vb[sr, vc]
                att = _dot(jnp.concatenate([q_even[sr, pc], q_odd[sr, pc]], axis=0), k_et)
                att_a = jnp.where(causal, att[0:SUB_ROWS], 0.0).astype(BF16)
                att_b = jnp.where(causal, att[SUB_ROWS:2 * SUB_ROWS], 0.0).astype(BF16)
                o_intra = jnp.concatenate(
                    [_dot(att_a, v2[:, 0:DV_B]), _dot(att_b, v2[:, DV_B:2 * DV_B])], axis=1)
                for c in range(SUB_ROWS // chunk):
                    c0 = sg * SUB_ROWS + c * chunk
                    row0 = g * gr + c0
                    n = row0 // tl
                    s_bd = st_scr[n, p]
                    o_inter = _dot(q_pair[c0:c0 + chunk, pc], s_bd.astype(BF16))
                    o_scr[row0:row0 + chunk, vc] = o_intra[c * chunk:(c + 1) * chunk] + o_inter
                    upd = _dot(jnp.where(in_chunk[c], k_lt, 0.0).astype(BF16), v2)
                    dcol = jnp.broadcast_to(dec[c0:c0 + 1, pc], (LANES, LANES)).T
                    st_scr[n, p] = (jnp.concatenate([dcol, dcol], axis=1) * s_bd
                                    + jnp.where(on_diag, upd, 0.0))

    for hh in range(H_B):
        vc = slice(hh * DV_B, (hh + 1) * DV_B)
        oh = o_scr[:, vc]
        ms = jnp.mean(oh * oh, axis=-1, keepdims=True)
        on = oh * lax.rsqrt(ms + EPS) * hg_ref[:, vc]
        yb = on * _silu(_cols(proj, every, O_BG + hh * DV_B, O_BG + (hh + 1) * DV_B))
        y_scr[:, D_A + hh * DV_B:D_A + (hh + 1) * DV_B] = yb.astype(BF16)

    y = _dot(y_scr[...], wout_ref[...])
    tr = slice(j * tl, (j + 1) * tl)
    xo_ref[:, tr, :] = x_ref[:, tr, :] + mod_ref[2] * y.reshape(nb, tl, D_MODEL)


def _even_kernel(x_ref, mod_ref, cst_ref, gst_ref, ng_ref, win_ref, cw_ref, cb_ref, lng_ref,
                 lnb_ref, gw2_ref, gb_ref, hg_ref, wout_ref, tri_ref, ones_ref,
                 xo_ref, co_ref, go_ref,
                 proj_scr, cbuf, obuf, hist_scr, st_scr, o_scr, y_scr, *, nb, tl, tps, chunk):
    step = pl.program_id(1)
    last = pl.num_programs(1) - 1

    @pl.when(step == 0)
    def _init():
        hist_scr[...] = jnp.zeros_like(hist_scr)
        hist_scr[:, SEG - CONV_PAD:SEG, :] = cst_ref[...]
        z = jnp.zeros((DK_B, DV_B), F32)
        for n in range(nb):
            for p in range(H_B // 2):
                top = jnp.concatenate([gst_ref[n, 2 * p], z], axis=1)
                bot = jnp.concatenate([z, gst_ref[n, 2 * p + 1]], axis=1)
                st_scr[n, p] = jnp.concatenate([top, bot], axis=0)

    conv = functools.partial(
        _even_conv, cw_ref=cw_ref, cb_ref=cb_ref, lng_ref=lng_ref, lnb_ref=lnb_ref, cbuf=cbuf,
        obuf=obuf, hist_scr=hist_scr, y_scr=y_scr, tl=tl)
    rest = functools.partial(
        _even_rest, x_ref=x_ref, mod_ref=mod_ref, gw2_ref=gw2_ref, gb_ref=gb_ref, hg_ref=hg_ref,
        wout_ref=wout_ref, tri_ref=tri_ref, ones_ref=ones_ref, xo_ref=xo_ref, st_scr=st_scr,
        o_scr=o_scr, y_scr=y_scr, nb=nb, tl=tl, chunk=chunk)
    _run_tiles(x_ref, mod_ref, ng_ref, win_ref, proj_scr, conv, rest, tl=tl, tps=tps)

    @pl.when(step == last)
    def _fin():
        co_ref[...] = hist_scr[:, SEG - CONV_PAD:SEG, :]
        for n in range(nb):
            for p in range(H_B // 2):
                go_ref[n, 2 * p] = st_scr[n, p, 0:DK_B, 0:DV_B]
                go_ref[n, 2 * p + 1] = st_scr[n, p, DK_B:2 * DK_B, DV_B:2 * DV_B]


def _decay_matrices(rows, chunk):
    i = np.arange(rows)
    same = (i[:, None] // chunk) == (i[None, :] // chunk)
    tri = same & (i[:, None] >= i[None, :])
    return jnp.asarray(tri, BF16), jnp.asarray(same, BF16)


def _even_call(x, mods, conv_st, gla_st, ng, w_in, cw, cb, lng, lnb, gw2, gb, hg, w_out, *,
               layer, mod_row0, state_layer, nb, tl, tps):
    b, l, d = x.shape
    e = layer // 2
    assert mod_row0 % nb == 0
    mod_blk0 = mod_row0 // nb
    chunk = min(GLA_CHUNK, l)
    m = nb * tl
    gr = min(m, GROUP_ROWS)
    n_tiles = l // tl
    assert b % nb == 0 and l % (tl * tps) == 0 and tl % chunk == 0 and m % gr == 0 and gr % chunk == 0
    assert tl % SEG == 0 and m % (SEG * SEGS) == 0
    assert gr % SUB_ROWS == 0 and SUB_ROWS % chunk == 0 and 2 * DK_B == LANES
    tri, ones_bd = _decay_matrices(gr, chunk)
    kern = functools.partial(_even_kernel, nb=nb, tl=tl, tps=tps, chunk=chunk)
    groups = m // (SEG * SEGS)
    return pl.pallas_call(
        kern,
        grid=(b // nb, n_tiles // tps),
        in_specs=[
            pl.BlockSpec((nb, tps * tl, d), lambda i, s: (i, s, 0)),
            pl.BlockSpec((None, 3, nb, 1, d), lambda i, s: (layer, 0, mod_blk0 + i, 0, 0)),
            pl.BlockSpec((None, nb, CONV_PAD, D_A), lambda i, s: (state_layer, i, 0, 0)),
            pl.BlockSpec((None, nb, H_B, DK_B, DV_B), lambda i, s: (state_layer, i, 0, 0, 0)),
            _layer_spec((1, d), layer),
            _layer_spec((d, EV_IN_PAD), e),
            _layer_spec((CONV_W, D_A), e),
            _layer_spec((1, D_A), e),
            _layer_spec((1, D_A), e),
            _layer_spec((1, D_A), e),
            _layer_spec((LANES, D_BK), e),
            _layer_spec((1, D_BK), e),
            _layer_spec((1, D_BV), e),
            _layer_spec((D_A + D_BV, d), e),
            _const_spec((gr, gr)),
            _const_spec((gr, gr)),
        ],
        out_specs=[
            pl.BlockSpec((nb, tps * tl, d), lambda i, s: (i, s, 0)),
            pl.BlockSpec((nb, CONV_PAD, D_A), lambda i, s: (i, 0, 0)),
            pl.BlockSpec((nb, H_B, DK_B, DV_B), lambda i, s: (i, 0, 0, 0)),
        ],
        out_shape=[
            jax.ShapeDtypeStruct((b, l, d), F32),
            jax.ShapeDtypeStruct((b, CONV_PAD, D_A), F32),
            jax.ShapeDtypeStruct((b, H_B, DK_B, DV_B), F32),
        ],
        scratch_shapes=[
            pltpu.VMEM((min(tps, 2), -(-EV_IN_PAD // MXU_COLS), m, MXU_COLS), F32),
            pltpu.VMEM((groups, N_SLAB, SEGS * SEG_PITCH, LANES), F32),
            pltpu.VMEM((groups, N_SLAB, SEGS * OUT_PITCH, LANES), F32),
            pltpu.VMEM((nb, SEG, D_A), F32),
            pltpu.VMEM((nb, H_B // 2, LANES, 2 * DV_B), F32),
            pltpu.VMEM((m, D_BV), F32),
            pltpu.VMEM((m, D_A + D_BV), BF16),
        ],
        compiler_params=pltpu.CompilerParams(
            dimension_semantics=("arbitrary", "arbitrary"), vmem_limit_bytes=VMEM_LIMIT),
        name="even_layer",
    )(x, mods, conv_st, gla_st, ng, w_in, cw, cb, lng, lnb, gw2, gb, hg, w_out, tri, ones_bd)


def _odd_pool(proj, j, g, vbuf, pbuf, phist_scr, p_scr, *, tl, tps, pos0):
    tile_pos = pos0 + (pl.program_id(1) * tps + j) * tl
    segs_per_seq = tl // SEG
    prev = None
    for i in range(SEGS):
        gi = g * SEGS + i
        n, si = divmod(gi, segs_per_seq)
        blk = _cols(proj, slice(gi * SEG, (gi + 1) * SEG), 0, D_C)
        halo = phist_scr[n] if prev is None or si == 0 else prev[SEG - POOL_HIST:SEG, :]
        for s in range(P_SLAB):
            ls = slice(s * LANES, (s + 1) * LANES)
            vbuf[g, s,i * POOL_PITCH:i * POOL_PITCH + POOL_HIST, :] = halo[:, ls]
            vbuf[g, s,i * POOL_PITCH + POOL_HIST:i * POOL_PITCH + POOL_HIST + SEG, :] = blk[:, ls]
        if i == SEGS - 1 or si == segs_per_seq - 1:
            phist_scr[n] = blk[SEG - POOL_HIST:SEG, :]
        prev = blk

    sub = lax.broadcasted_iota(jnp.int32, (SEGS, LANES), 0)
    seg_pos1 = tile_pos + 1 + ((sub + g * SEGS) % segs_per_seq) * SEG
    for wi, w in enumerate(POOL_WINDOWS):
        for s in range(wi * (POOL_GW // LANES), (wi + 1) * (POOL_GW // LANES)):
            lo = -(w - 1)
            cur = {k: vbuf[g, s,pl.ds(POOL_HIST + k, SEGS, stride=POOL_PITCH), :]
                   for k in range(lo, SEG)}
            xk = cur
            d = 1
            while d < w:
                lo += d
                cur = {k: cur[k] + cur[k - d] for k in range(lo, SEG)}
                d *= 2
            for k in range(SEG):
                if k >= w - 1:
                    p = cur[k] * (1.0 / w) - xk[k]
                else:
                    cnt = jnp.minimum(seg_pos1 + k, w).astype(F32)
                    p = cur[k] / cnt - xk[k]
                pbuf[g, s, pl.ds(k, SEGS, stride=OUT_PITCH), :] = p

    for i in range(SEGS):
        gi = g * SEGS + i
        rows = slice(gi * SEG, (gi + 1) * SEG)
        for s in range(P_SLAB):
            p_scr[rows, s * LANES:(s + 1) * LANES] = pbuf[
                g, s, i * OUT_PITCH:i * OUT_PITCH + SEG, :].astype(BF16)


def _odd_rest(proj, j, x_ref, mod_ref, gw_ref, gb_ref, sc_ref, wout_ref, fg_ref, xo_ref,
              p_scr, z_scr, *, nb, tl, final):
    for gi in range(len(POOL_WINDOWS)):
        cs = slice(gi * POOL_GW, (gi + 1) * POOL_GW)
        pg = _dot(p_scr[:, cs], gw_ref[gi]) + gb_ref[:, cs]
        z = pg * sc_ref[:, cs] * _silu(
            _cols(proj, slice(None), D_C + gi * POOL_GW, D_C + (gi + 1) * POOL_GW))
        z_scr[:, cs] = z.astype(BF16)

    y = _dot(z_scr[...], wout_ref[...])
    tr = slice(j * tl, (j + 1) * tl)
    xo = x_ref[:, tr, :] + mod_ref[2] * y.reshape(nb, tl, D_MODEL)
    if final:
        ms = jnp.mean(xo * xo, axis=-1, keepdims=True)
        xo = xo * lax.rsqrt(ms + EPS) * fg_ref[...]
    xo_ref[:, tr, :] = xo


def _odd_kernel(x_ref, mod_ref, pst_ref, ng_ref, win_ref, gw_ref, gb_ref, sc_ref, wout_ref, fg_ref,
                xo_ref, po_ref,
                proj_scr, vbuf, pbuf, phist_scr, p_scr, z_scr, *, nb, tl, tps, pos0, final):
    step = pl.program_id(1)
    last = pl.num_programs(1) - 1

    @pl.when(step == 0)
    def _init():
        phist_scr[...] = jnp.zeros_like(phist_scr)
        phist_scr[:, POOL_HIST - POOL_PAD:POOL_HIST, :] = pst_ref[...]

    pool = functools.partial(
        _odd_pool, vbuf=vbuf, pbuf=pbuf, phist_scr=phist_scr, p_scr=p_scr, tl=tl, tps=tps, pos0=pos0)
    rest = functools.partial(
        _odd_rest, x_ref=x_ref, mod_ref=mod_ref, gw_ref=gw_ref, gb_ref=gb_ref, sc_ref=sc_ref,
        wout_ref=wout_ref, fg_ref=fg_ref, xo_ref=xo_ref, p_scr=p_scr, z_scr=z_scr,
        nb=nb, tl=tl, final=final)
    _run_tiles(x_ref, mod_ref, ng_ref, win_ref, proj_scr, pool, rest, tl=tl, tps=tps)

    @pl.when(step == last)
    def _fin():
        po_ref[...] = phist_scr[:, POOL_HIST - POOL_PAD:POOL_HIST, :]


def _odd_call(x, mods, pool_st, ng, w_in, gw, gb, sc, w_out, fg, *,
              layer, mod_row0, state_layer, nb, tl, tps, pos0):
    b, l, d = x.shape
    o = layer // 2
    final = layer == DEPTH - 1
    assert mod_row0 % nb == 0
    mod_blk0 = mod_row0 // nb
    m = nb * tl
    n_tiles = l // tl
    assert b % nb == 0 and l % (tl * tps) == 0 and tl % SEG == 0 and m % (SEG * SEGS) == 0
    kern = functools.partial(_odd_kernel, nb=nb, tl=tl, tps=tps, pos0=pos0, final=final)
    groups = m // (SEG * SEGS)
    return pl.pallas_call(
        kern,
        grid=(b // nb, n_tiles // tps),
        in_specs=[
            pl.BlockSpec((nb, tps * tl, d), lambda i, s: (i, s, 0)),
            pl.BlockSpec((None, 3, nb, 1, d), lambda i, s: (layer, 0, mod_blk0 + i, 0, 0)),
            pl.BlockSpec((None, nb, POOL_PAD, D_C), lambda i, s: (state_layer, i, 0, 0)),
            _layer_spec((1, d), layer),
            _layer_spec((d, 2 * D_C), o),
            _layer_spec((len(POOL_WINDOWS), POOL_GW, POOL_GW), o),
            _layer_spec((1, D_C), o),
            _layer_spec((1, D_C), o),
            _layer_spec((D_C, d), o),
            _const_spec((1, d)),
        ],
        out_specs=[
            pl.BlockSpec((nb, tps * tl, d), lambda i, s: (i, s, 0)),
            pl.BlockSpec((nb, POOL_PAD, D_C), lambda i, s: (i, 0, 0)),
        ],
        out_shape=[
            jax.ShapeDtypeStruct((b, l, d), F32),
            jax.ShapeDtypeStruct((b, POOL_PAD, D_C), F32),
        ],
        scratch_shapes=[
            pltpu.VMEM((min(tps, 2), 2 * D_C // MXU_COLS, m, MXU_COLS), F32),
            pltpu.VMEM((groups, P_SLAB, SEGS * POOL_PITCH, LANES), F32),
            pltpu.VMEM((groups, P_SLAB, SEGS * OUT_PITCH, LANES), F32),
            pltpu.VMEM((nb, POOL_HIST, D_C), F32),
            pltpu.VMEM((m, D_C), BF16),
            pltpu.VMEM((m, D_C), BF16),
        ],
        compiler_params=pltpu.CompilerParams(
            dimension_semantics=("arbitrary", "arbitrary"), vmem_limit_bytes=VMEM_LIMIT),
        name="odd_layer",
    )(x, mods, pool_st, ng, w_in, gw, gb, sc, w_out, fg)


PROMPT_TILE = dict(nb=4, tl=256, tps=1)
SAMPLE_TILE = dict(nb=16, tl=32, tps=1)


def kernel(x_prompt, x_sample, c_prompt, c_sample, state_conv, state_gla, state_pool, norm_g, ada_w, ada_b, ev_w_in, ev_conv_w, ev_conv_b, ev_ln_g, ev_ln_b, ev_gate_w2, ev_gate_b, ev_head_g, ev_w_out, od_w_in, od_group_w, od_group_b, od_scale, od_w_out, final_g):
    b_p = x_prompt.shape[0]
    b_s = x_sample.shape[0]
    n_even = ev_w_in.shape[0]
    n_odd = od_w_in.shape[0]

    c_all = jnp.concatenate([c_sample, c_prompt], axis=0)
    bp = -(-c_all.shape[0] // SUBLANES) * SUBLANES
    c_all = jnp.pad(c_all, ((0, bp - c_all.shape[0]), (0, 0)))
    mods = _ada_call(c_all, ada_w, ada_b).reshape(DEPTH, 3, bp, 1, D_MODEL)

    ev_w_in_b = jnp.pad(ev_w_in.astype(BF16), ((0, 0), (0, 0), (0, EV_IN_PAD - EV_IN)))
    ev_gw2_b = jnp.pad(ev_gate_w2.astype(BF16), ((0, 0), (0, LANES - GATE_RANK), (0, 0)))
    ev_w_out_b = ev_w_out.astype(BF16)
    od_w_in_b = od_w_in.astype(BF16)
    od_gw_b = od_group_w.astype(BF16)
    od_w_out_b = od_w_out.astype(BF16)

    zero_conv = jnp.zeros((1, b_p, CONV_PAD, D_A), F32)
    zero_gla = jnp.zeros((1, b_p, H_B, DK_B, DV_B), F32)
    zero_pool = jnp.zeros((1, b_p, POOL_PAD, D_C), F32)

    def rows(a):
        return a.reshape(a.shape[0], 1, a.shape[1])

    ng = rows(norm_g)
    ev_w = (ng, ev_w_in_b, ev_conv_w, rows(ev_conv_b), rows(ev_ln_g), rows(ev_ln_b), ev_gw2_b,
            rows(ev_gate_b), rows(ev_head_g), ev_w_out_b)
    od_w = (ng, od_w_in_b, od_gw_b, rows(od_group_b), rows(od_scale), od_w_out_b,
            final_g.reshape(1, D_MODEL))
    prompt = dict(mod_row0=b_s, state_layer=0, **PROMPT_TILE)

    xp, xs = x_prompt, x_sample
    conv_p, gla_p, pool_p, conv_s, gla_s, pool_s = [], [], [], [], [], []
    for l in range(DEPTH):
        sample = dict(mod_row0=0, state_layer=l // 2, **SAMPLE_TILE)
        if l % 2 == 0:
            xp, cp, gp = _even_call(xp, mods, zero_conv, zero_gla, *ev_w, layer=l, **prompt)
            xs, cs, gs = _even_call(xs, mods, state_conv, state_gla, *ev_w, layer=l, **sample)
            conv_p.append(cp)
            gla_p.append(gp)
            conv_s.append(cs)
            gla_s.append(gs)
        else:
            xp, pp = _odd_call(xp, mods, zero_pool, *od_w, layer=l, pos0=0, **prompt)
            xs, ps = _odd_call(xs, mods, state_pool, *od_w, layer=l, pos0=PAST_LEN, **sample)
            pool_p.append(pp)
            pool_s.append(ps)
    assert n_even == len(conv_p) and n_odd == len(pool_p)
    return (xp, xs, jnp.stack(conv_p), jnp.stack(gla_p), jnp.stack(pool_p),
            jnp.stack(conv_s), jnp.stack(gla_s), jnp.stack(pool_s))
```

```python
import functools

import numpy as np
import jax
import jax.numpy as jnp
from jax import lax
from jax.experimental import pallas as pl
from jax.experimental.pallas import tpu as pltpu

F32 = jnp.float32
BF16 = jnp.bfloat16

LANES = 128
SUBLANES = 8
VMEM_LIMIT = 56 * 1024 * 1024
MXU_COLS = 256
GROUP_ROWS = MXU_COLS

D_MODEL = 1024
DEPTH = 4
EPS = 1e-6
PAST_LEN = 2048
LOG2E = 1.4426950408889634

D_A = D_MODEL // 2
CONV_W = 31
CONV_PAD = CONV_W - 1
SEG = 32
SEGS = SUBLANES
SEG_PITCH = 68
OUT_PITCH = 36
N_SLAB = D_A // LANES
CONV_KB = 4

H_B = 4
DK_B = D_MODEL // 16
DV_B = D_MODEL // 8
D_BK = H_B * DK_B
D_BV = H_B * DV_B
GATE_RANK = 16
GATE_TAU = 16.0
GLA_CHUNK = 64
SUB_ROWS = LANES

D_C = D_MODEL
POOL_WINDOWS = (2, 4, 8, 16)
POOL_GW = D_C // len(POOL_WINDOWS)
POOL_PAD = 15
POOL_HIST = 16
POOL_PITCH = 52
P_SLAB = D_C // LANES

O_VAL, O_GLU, O_GATE = 0, D_A, 2 * D_A
O_Q = 3 * D_A
O_K = O_Q + D_BK
O_V = O_K + D_BK
O_BG = O_V + D_BV
O_LR = O_BG + D_BV
EV_IN = O_LR + GATE_RANK
EV_IN_PAD = O_LR + LANES


def _sigmoid(x):
    return 1.0 / (1.0 + jnp.exp2(x * (-LOG2E)))


def _silu(x):
    return x * _sigmoid(x)


def _split_bf16(x):
    hi = x.astype(BF16)
    lo = (x - hi.astype(F32)).astype(BF16)
    return hi, lo


def _dot(a, b):
    return jnp.dot(a, b, preferred_element_type=F32)


def _const_spec(shape):
    nd = len(shape)
    return pl.BlockSpec(shape, lambda *_: (0,) * nd, pipeline_mode=pl.Buffered(1))


def _layer_spec(shape, layer):
    nd = len(shape)
    return pl.BlockSpec((None,) + tuple(shape), lambda *_: (layer,) + (0,) * nd,
                        pipeline_mode=pl.Buffered(1))


def _ada_kernel(c_ref, w_ref, b_ref, o_ref):
    cs = _silu(c_ref[...])
    acc = _dot(cs.astype(BF16), w_ref[0].astype(BF16))
    for j in range(3):
        o_ref[0, j] = acc[:, j * D_MODEL:(j + 1) * D_MODEL] + b_ref[0, j]


def _ada_call(c_all, ada_w, ada_b):
    bp = c_all.shape[0]
    return pl.pallas_call(
        _ada_kernel,
        grid=(DEPTH,),
        in_specs=[
            pl.BlockSpec((bp, D_MODEL), lambda l: (0, 0)),
            pl.BlockSpec((1, D_MODEL, 3 * D_MODEL), lambda l: (l, 0, 0)),
            pl.BlockSpec((1, 3, 1, D_MODEL), lambda l: (l, 0, 0, 0)),
        ],
        out_specs=pl.BlockSpec((1, 3, bp, D_MODEL), lambda l: (l, 0, 0, 0)),
        out_shape=jax.ShapeDtypeStruct((DEPTH, 3, bp, D_MODEL), F32),
        compiler_params=pltpu.CompilerParams(
            dimension_semantics=("arbitrary",), vmem_limit_bytes=VMEM_LIMIT),
        name="ada_mod",
    )(c_all, ada_w, ada_b.reshape(DEPTH, 3, 1, D_MODEL))


def _modulated_input(x3, mod_ref, ng_ref):
    nb, tl, d = x3.shape
    shift = mod_ref[0]
    scale = mod_ref[1]
    ms = jnp.mean(x3 * x3, axis=-1, keepdims=True)
    gs = ng_ref[...] * (1.0 + scale)
    h = (x3 * lax.rsqrt(ms + EPS)) * gs + shift
    return h.astype(BF16).reshape(nb * tl, d)


def _cols(proj, rows, c0, c1):
    parts = []
    while c0 < c1:
        p, off = divmod(c0, MXU_COLS)
        w = min(MXU_COLS - off, c1 - c0)
        parts.append(proj[p, rows, off:off + w])
        c0 += w
    return parts[0] if len(parts) == 1 else jnp.concatenate(parts, axis=1)


def _run_tiles(x_ref, mod_ref, ng_ref, win_ref, proj_scr, group_stage, rest_stage, *, tl, tps):
    width = win_ref.shape[1]
    for j in range(tps):
        hb = _modulated_input(x_ref[:, j * tl:(j + 1) * tl, :], mod_ref, ng_ref)
        proj = proj_scr.at[j % 2]
        groups = hb.shape[0] // GROUP_ROWS
        for g in range(groups):
            rows = slice(g * GROUP_ROWS, (g + 1) * GROUP_ROWS)
            for p, c0 in enumerate(range(0, width, MXU_COLS)):
                c1 = min(c0 + MXU_COLS, width)
                proj[p, rows, 0:c1 - c0] = _dot(hb[rows], win_ref[:, c0:c1])
        for g in range(groups):
            group_stage(proj, j, g)
        rest_stage(proj, j)


def _even_conv(proj, j, g, cw_ref, cb_ref, lng_ref, lnb_ref, cbuf, obuf, hist_scr, y_scr,
               *, tl):
    del j
    segs_per_seq = tl // SEG
    prev = None
    for i in range(SEGS):
        gi = g * SEGS + i
        n, si = divmod(gi, segs_per_seq)
        rows = slice(gi * SEG, (gi + 1) * SEG)
        blk = _cols(proj, rows, O_VAL, O_VAL + D_A) * _sigmoid(_cols(proj, rows, O_GLU, O_GLU + D_A))
        halo = hist_scr[n] if prev is None or si == 0 else prev
        for s in range(N_SLAB):
            ls = slice(s * LANES, (s + 1) * LANES)
            cbuf[g, s,i * SEG_PITCH:i * SEG_PITCH + SEG, :] = halo[:, ls]
            cbuf[g, s,i * SEG_PITCH + SEG:i * SEG_PITCH + 2 * SEG, :] = blk[:, ls]
        if i == SEGS - 1 or si == segs_per_seq - 1:
            hist_scr[n] = blk
        prev = blk

    inv_da = 1.0 / D_A
    for kb in range(SEG // CONV_KB):
        acc = [[jnp.broadcast_to(cb_ref[:, s * LANES:(s + 1) * LANES], (SEGS, LANES))
                for s in range(N_SLAB)] for _ in range(CONV_KB)]
        for jt in range(CONV_W):
            for s in range(N_SLAB):
                wj = cw_ref[jt:jt + 1, s * LANES:(s + 1) * LANES]
                for kk in range(CONV_KB):
                    start = SEG - CONV_PAD + kb * CONV_KB + kk + jt
                    acc[kk][s] = acc[kk][s] + wj * cbuf[g, s, pl.ds(start, SEGS, stride=SEG_PITCH), :]
        for kk in range(CONV_KB):
            a = acc[kk]
            mu = jnp.sum(a[0] + a[1] + a[2] + a[3], axis=-1, keepdims=True) * inv_da
            xc = [v - mu for v in a]
            sq = xc[0] * xc[0] + xc[1] * xc[1] + xc[2] * xc[2] + xc[3] * xc[3]
            rstd = lax.rsqrt(jnp.sum(sq, axis=-1, keepdims=True) * inv_da + EPS)
            for s in range(N_SLAB):
                ls = slice(s * LANES, (s + 1) * LANES)
                yn = xc[s] * rstd * lng_ref[:, ls] + lnb_ref[:, ls]
                obuf[g, s, pl.ds(kb * CONV_KB + kk, SEGS, stride=OUT_PITCH), :] = _silu(yn)

    for i in range(SEGS):
        gi = g * SEGS + i
        rows = slice(gi * SEG, (gi + 1) * SEG)
        for s in range(N_SLAB):
            ya = obuf[g, s,i * OUT_PITCH:i * OUT_PITCH + SEG, :] * _silu(
                _cols(proj, rows, O_GATE + s * LANES, O_GATE + (s + 1) * LANES))
            y_scr[rows, s * LANES:(s + 1) * LANES] = ya.astype(BF16)


def _even_rest(proj, j, x_ref, mod_ref, gw2_ref, gb_ref, hg_ref, wout_ref, tri_ref, ones_ref,
               xo_ref, st_scr, o_scr, y_scr, *, nb, tl, chunk):
    m = nb * tl
    gr = min(m, GROUP_ROWS)
    lane = lax.broadcasted_iota(jnp.int32, (1, LANES), 1)
    lo_half = lane < DK_B

    every = slice(None)
    pre = _dot(_cols(proj, every, O_LR, O_LR + LANES).astype(BF16), gw2_ref[...]) + gb_ref[...]
    la = (jnp.minimum(pre, 0.0) - jnp.log1p(jnp.exp(-jnp.abs(pre)))) * (1.0 / GATE_TAU)
    la_hi, la_lo = _split_bf16(la)
    tri = tri_ref[...]
    ones_bd = ones_ref[...]
    causal = tri_ref[0:SUB_ROWS, 0:SUB_ROWS].astype(F32) > 0.5
    srow = lax.broadcasted_iota(jnp.int32, (LANES, 2 * DV_B), 0)
    scol = lax.broadcasted_iota(jnp.int32, (LANES, 2 * DV_B), 1)
    on_diag = (srow < DK_B) == (scol < DV_B)
    in_chunk = [(lane >= c * chunk) & (lane < (c + 1) * chunk) for c in range(SUB_ROWS // chunk)]
    for g in range(m // gr):
        rows = slice(g * gr, (g + 1) * gr)
        b = _dot(tri, la_hi[rows]) + _dot(tri, la_lo[rows])
        bl = _dot(ones_bd, la_hi[rows]) + _dot(ones_bd, la_lo[rows])
        for sg in range(gr // SUB_ROWS):
            sr = slice(sg * SUB_ROWS, (sg + 1) * SUB_ROWS)
            ar = slice(g * gr + sg * SUB_ROWS, g * gr + (sg + 1) * SUB_ROWS)
            for p in range(H_B // 2):
                pc = slice(p * LANES, (p + 1) * LANES)
                vc = slice(p * 2 * DV_B, (p + 1) * 2 * DV_B)
                b_s = b[sr, pc]
                bl_s = bl[sr, pc]
                k = _cols(proj, ar, O_K + p * LANES, O_K + (p + 1) * LANES)
                q_e = _cols(proj, ar, O_Q + p * LANES, O_Q + (p + 1) * LANES) * (DK_B ** -0.5) * jnp.exp(b_s)
                q_pair = q_e.astype(BF16)
                q_even = jnp.where(lo_half, q_e, 0.0).astype(BF16)
                q_odd = jnp.where(lo_half, 0.0, q_e).astype(BF16)
                k_et = (k * jnp.exp(-b_s)).T.astype(BF16)
                k_lt = (k * jnp.exp(bl_s - b_s)).T
                dec = jnp.exp(bl_s)
                v2 = _cols(proj, ar, O_V + p * 2 * DV_B, O_V + (p + 1) * 2 * DV_B).astype(BF16)
                att = _dot(jnp.concatenate([q_even, q_odd], axis=0), k_et)
                att_a = jnp.where(causal, att[0:SUB_ROWS], 0.0).astype(BF16)
                att_b = jnp.where(causal, att[SUB_ROWS:2 * SUB_ROWS], 0.0).astype(BF16)
                o_intra = jnp.concatenate(
                    [_dot(att_a, v2[:, 0:DV_B]), _dot(att_b, v2[:, DV_B:2 * DV_B])], axis=1)
                for c in range(SUB_ROWS // chunk):
                    c0 = sg * SUB_ROWS + c * chunk
                    row0 = g * gr + c0
                    n = row0 // tl
                    s_bd = st_scr[n, p]
                    o_inter = _dot(q_pair[c * chunk:(c + 1) * chunk], s_bd.astype(BF16))
                    o_scr[row0:row0 + chunk, vc] = o_intra[c * chunk:(c + 1) * chunk] + o_inter
                    upd = _dot(jnp.where(in_chunk[c], k_lt, 0.0).astype(BF16), v2)
                    dcol = jnp.broadcast_to(dec[c * chunk:c * chunk + 1], (LANES, LANES)).T
                    st_scr[n, p] = (jnp.concatenate([dcol, dcol], axis=1) * s_bd
                                    + jnp.where(on_diag, upd, 0.0))

    for hh in range(H_B):
        vc = slice(hh * DV_B, (hh + 1) * DV_B)
        oh = o_scr[:, vc]
        ms = jnp.mean(oh * oh, axis=-1, keepdims=True)
        on = oh * lax.rsqrt(ms + EPS) * hg_ref[:, vc]
        yb = on * _silu(_cols(proj, every, O_BG + hh * DV_B, O_BG + (hh + 1) * DV_B))
        y_scr[:, D_A + hh * DV_B:D_A + (hh + 1) * DV_B] = yb.astype(BF16)

    y = _dot(y_scr[...], wout_ref[...])
    tr = slice(j * tl, (j + 1) * tl)
    xo_ref[:, tr, :] = x_ref[:, tr, :] + mod_ref[2] * y.reshape(nb, tl, D_MODEL)


def _even_kernel(x_ref, mod_ref, cst_ref, gst_ref, ng_ref, win_ref, cw_ref, cb_ref, lng_ref,
                 lnb_ref, gw2_ref, gb_ref, hg_ref, wout_ref, tri_ref, ones_ref,
                 xo_ref, co_ref, go_ref,
                 proj_scr, cbuf, obuf, hist_scr, st_scr, o_scr, y_scr, *, nb, tl, tps, chunk):
    step = pl.program_id(1)
    last = pl.num_programs(1) - 1

    @pl.when(step == 0)
    def _init():
        hist_scr[...] = jnp.zeros_like(hist_scr)
        hist_scr[:, SEG - CONV_PAD:SEG, :] = cst_ref[...]
        z = jnp.zeros((DK_B, DV_B), F32)
        for n in range(nb):
            for p in range(H_B // 2):
                top = jnp.concatenate([gst_ref[n, 2 * p], z], axis=1)
                bot = jnp.concatenate([z, gst_ref[n, 2 * p + 1]], axis=1)
                st_scr[n, p] = jnp.concatenate([top, bot], axis=0)

    conv = functools.partial(
        _even_conv, cw_ref=cw_ref, cb_ref=cb_ref, lng_ref=lng_ref, lnb_ref=lnb_ref, cbuf=cbuf,
        obuf=obuf, hist_scr=hist_scr, y_scr=y_scr, tl=tl)
    rest = functools.partial(
        _even_rest, x_ref=x_ref, mod_ref=mod_ref, gw2_ref=gw2_ref, gb_ref=gb_ref, hg_ref=hg_ref,
        wout_ref=wout_ref, tri_ref=tri_ref, ones_ref=ones_ref, xo_ref=xo_ref, st_scr=st_scr,
        o_scr=o_scr, y_scr=y_scr, nb=nb, tl=tl, chunk=chunk)
    _run_tiles(x_ref, mod_ref, ng_ref, win_ref, proj_scr, conv, rest, tl=tl, tps=tps)

    @pl.when(step == last)
    def _fin():
        co_ref[...] = hist_scr[:, SEG - CONV_PAD:SEG, :]
        for n in range(nb):
            for p in range(H_B // 2):
                go_ref[n, 2 * p] = st_scr[n, p, 0:DK_B, 0:DV_B]
                go_ref[n, 2 * p + 1] = st_scr[n, p, DK_B:2 * DK_B, DV_B:2 * DV_B]


def _decay_matrices(rows, chunk):
    i = np.arange(rows)
    same = (i[:, None] // chunk) == (i[None, :] // chunk)
    tri = same & (i[:, None] >= i[None, :])
    return jnp.asarray(tri, BF16), jnp.asarray(same, BF16)


def _even_call(x, mods, conv_st, gla_st, ng, w_in, cw, cb, lng, lnb, gw2, gb, hg, w_out, *,
               layer, mod_row0, state_layer, nb, tl, tps):
    b, l, d = x.shape
    e = layer // 2
    assert mod_row0 % nb == 0
    mod_blk0 = mod_row0 // nb
    chunk = min(GLA_CHUNK, l)
    m = nb * tl
    gr = min(m, GROUP_ROWS)
    n_tiles = l // tl
    assert b % nb == 0 and l % (tl * tps) == 0 and tl % chunk == 0 and m % gr == 0 and gr % chunk == 0
    assert tl % SEG == 0 and m % (SEG * SEGS) == 0
    assert gr % SUB_ROWS == 0 and SUB_ROWS % chunk == 0 and 2 * DK_B == LANES
    tri, ones_bd = _decay_matrices(gr, chunk)
    kern = functools.partial(_even_kernel, nb=nb, tl=tl, tps=tps, chunk=chunk)
    groups = m // (SEG * SEGS)
    return pl.pallas_call(
        kern,
        grid=(b // nb, n_tiles // tps),
        in_specs=[
            pl.BlockSpec((nb, tps * tl, d), lambda i, s: (i, s, 0)),
            pl.BlockSpec((None, 3, nb, 1, d), lambda i, s: (layer, 0, mod_blk0 + i, 0, 0)),
            pl.BlockSpec((None, nb, CONV_PAD, D_A), lambda i, s: (state_layer, i, 0, 0)),
            pl.BlockSpec((None, nb, H_B, DK_B, DV_B), lambda i, s: (state_layer, i, 0, 0, 0)),
            _layer_spec((1, d), layer),
            _layer_spec((d, EV_IN_PAD), e),
            _layer_spec((CONV_W, D_A), e),
            _layer_spec((1, D_A), e),
            _layer_spec((1, D_A), e),
            _layer_spec((1, D_A), e),
            _layer_spec((LANES, D_BK), e),
            _layer_spec((1, D_BK), e),
            _layer_spec((1, D_BV), e),
            _layer_spec((D_A + D_BV, d), e),
            _const_spec((gr, gr)),
            _const_spec((gr, gr)),
        ],
        out_specs=[
            pl.BlockSpec((nb, tps * tl, d), lambda i, s: (i, s, 0)),
            pl.BlockSpec((nb, CONV_PAD, D_A), lambda i, s: (i, 0, 0)),
            pl.BlockSpec((nb, H_B, DK_B, DV_B), lambda i, s: (i, 0, 0, 0)),
        ],
        out_shape=[
            jax.ShapeDtypeStruct((b, l, d), F32),
            jax.ShapeDtypeStruct((b, CONV_PAD, D_A), F32),
            jax.ShapeDtypeStruct((b, H_B, DK_B, DV_B), F32),
        ],
        scratch_shapes=[
            pltpu.VMEM((min(tps, 2), -(-EV_IN_PAD // MXU_COLS), m, MXU_COLS), F32),
            pltpu.VMEM((groups, N_SLAB, SEGS * SEG_PITCH, LANES), F32),
            pltpu.VMEM((groups, N_SLAB, SEGS * OUT_PITCH, LANES), F32),
            pltpu.VMEM((nb, SEG, D_A), F32),
            pltpu.VMEM((nb, H_B // 2, LANES, 2 * DV_B), F32),
            pltpu.VMEM((m, D_BV), F32),
            pltpu.VMEM((m, D_A + D_BV), BF16),
        ],
        compiler_params=pltpu.CompilerParams(
            dimension_semantics=("arbitrary", "arbitrary"), vmem_limit_bytes=VMEM_LIMIT),
        name="even_layer",
    )(x, mods, conv_st, gla_st, ng, w_in, cw, cb, lng, lnb, gw2, gb, hg, w_out, tri, ones_bd)


def _odd_pool(proj, j, g, vbuf, pbuf, phist_scr, p_scr, *, tl, tps, pos0):
    tile_pos = pos0 + (pl.program_id(1) * tps + j) * tl
    segs_per_seq = tl // SEG
    prev = None
    for i in range(SEGS):
        gi = g * SEGS + i
        n, si = divmod(gi, segs_per_seq)
        blk = _cols(proj, slice(gi * SEG, (gi + 1) * SEG), 0, D_C)
        halo = phist_scr[n] if prev is None or si == 0 else prev[SEG - POOL_HIST:SEG, :]
        for s in range(P_SLAB):
            ls = slice(s * LANES, (s + 1) * LANES)
            vbuf[g, s,i * POOL_PITCH:i * POOL_PITCH + POOL_HIST, :] = halo[:, ls]
            vbuf[g, s,i * POOL_PITCH + POOL_HIST:i * POOL_PITCH + POOL_HIST + SEG, :] = blk[:, ls]
        if i == SEGS - 1 or si == segs_per_seq - 1:
            phist_scr[n] = blk[SEG - POOL_HIST:SEG, :]
        prev = blk

    sub = lax.broadcasted_iota(jnp.int32, (SEGS, LANES), 0)
    seg_pos1 = tile_pos + 1 + ((sub + g * SEGS) % segs_per_seq) * SEG
    for wi, w in enumerate(POOL_WINDOWS):
        for s in range(wi * (POOL_GW // LANES), (wi + 1) * (POOL_GW // LANES)):
            lo = -(w - 1)
            cur = {k: vbuf[g, s,pl.ds(POOL_HIST + k, SEGS, stride=POOL_PITCH), :]
                   for k in range(lo, SEG)}
            xk = cur
            d = 1
            while d < w:
                lo += d
                cur = {k: cur[k] + cur[k - d] for k in range(lo, SEG)}
                d *= 2
            for k in range(SEG):
                if k >= w - 1:
                    p = cur[k] * (1.0 / w) - xk[k]
                else:
                    cnt = jnp.minimum(seg_pos1 + k, w).astype(F32)
                    p = cur[k] / cnt - xk[k]
                pbuf[g, s, pl.ds(k, SEGS, stride=OUT_PITCH), :] = p

    for i in range(SEGS):
        gi = g * SEGS + i
        rows = slice(gi * SEG, (gi + 1) * SEG)
        for s in range(P_SLAB):
            p_scr[rows, s * LANES:(s + 1) * LANES] = pbuf[
                g, s, i * OUT_PITCH:i * OUT_PITCH + SEG, :].astype(BF16)


def _odd_rest(proj, j, x_ref, mod_ref, gw_ref, gb_ref, sc_ref, wout_ref, fg_ref, xo_ref,
              p_scr, z_scr, *, nb, tl, final):
    for gi in range(len(POOL_WINDOWS)):
        cs = slice(gi * POOL_GW, (gi + 1) * POOL_GW)
        pg = _dot(p_scr[:, cs], gw_ref[gi]) + gb_ref[:, cs]
        z = pg * sc_ref[:, cs] * _silu(
            _cols(proj, slice(None), D_C + gi * POOL_GW, D_C + (gi + 1) * POOL_GW))
        z_scr[:, cs] = z.astype(BF16)

    y = _dot(z_scr[...], wout_ref[...])
    tr = slice(j * tl, (j + 1) * tl)
    xo = x_ref[:, tr, :] + mod_ref[2] * y.reshape(nb, tl, D_MODEL)
    if final:
        ms = jnp.mean(xo * xo, axis=-1, keepdims=True)
        xo = xo * lax.rsqrt(ms + EPS) * fg_ref[...]
    xo_ref[:, tr, :] = xo


def _odd_kernel(x_ref, mod_ref, pst_ref, ng_ref, win_ref, gw_ref, gb_ref, sc_ref, wout_ref, fg_ref,
                xo_ref, po_ref,
                proj_scr, vbuf, pbuf, phist_scr, p_scr, z_scr, *, nb, tl, tps, pos0, final):
    step = pl.program_id(1)
    last = pl.num_programs(1) - 1

    @pl.when(step == 0)
    def _init():
        phist_scr[...] = jnp.zeros_like(phist_scr)
        phist_scr[:, POOL_HIST - POOL_PAD:POOL_HIST, :] = pst_ref[...]

    pool = functools.partial(
        _odd_pool, vbuf=vbuf, pbuf=pbuf, phist_scr=phist_scr, p_scr=p_scr, tl=tl, tps=tps, pos0=pos0)
    rest = functools.partial(
        _odd_rest, x_ref=x_ref, mod_ref=mod_ref, gw_ref=gw_ref, gb_ref=gb_ref, sc_ref=sc_ref,
        wout_ref=wout_ref, fg_ref=fg_ref, xo_ref=xo_ref, p_scr=p_scr, z_scr=z_scr,
        nb=nb, tl=tl, final=final)
    _run_tiles(x_ref, mod_ref, ng_ref, win_ref, proj_scr, pool, rest, tl=tl, tps=tps)

    @pl.when(step == last)
    def _fin():
        po_ref[...] = phist_scr[:, POOL_HIST - POOL_PAD:POOL_HIST, :]


def _odd_call(x, mods, pool_st, ng, w_in, gw, gb, sc, w_out, fg, *,
              layer, mod_row0, state_layer, nb, tl, tps, pos0):
    b, l, d = x.shape
    o = layer // 2
    final = layer == DEPTH - 1
    assert mod_row0 % nb == 0
    mod_blk0 = mod_row0 // nb
    m = nb * tl
    n_tiles = l // tl
    assert b % nb == 0 and l % (tl * tps) == 0 and tl % SEG == 0 and m % (SEG * SEGS) == 0
    kern = functools.partial(_odd_kernel, nb=nb, tl=tl, tps=tps, pos0=pos0, final=final)
    groups = m // (SEG * SEGS)
    return pl.pallas_call(
        kern,
        grid=(b // nb, n_tiles // tps),
        in_specs=[
            pl.BlockSpec((nb, tps * tl, d), lambda i, s: (i, s, 0)),
            pl.BlockSpec((None, 3, nb, 1, d), lambda i, s: (layer, 0, mod_blk0 + i, 0, 0)),
            pl.BlockSpec((None, nb, POOL_PAD, D_C), lambda i, s: (state_layer, i, 0, 0)),
            _layer_spec((1, d), layer),
            _layer_spec((d, 2 * D_C), o),
            _layer_spec((len(POOL_WINDOWS), POOL_GW, POOL_GW), o),
            _layer_spec((1, D_C), o),
            _layer_spec((1, D_C), o),
            _layer_spec((D_C, d), o),
            _const_spec((1, d)),
        ],
        out_specs=[
            pl.BlockSpec((nb, tps * tl, d), lambda i, s: (i, s, 0)),
            pl.BlockSpec((nb, POOL_PAD, D_C), lambda i, s: (i, 0, 0)),
        ],
        out_shape=[
            jax.ShapeDtypeStruct((b, l, d), F32),
            jax.ShapeDtypeStruct((b, POOL_PAD, D_C), F32),
        ],
        scratch_shapes=[
            pltpu.VMEM((min(tps, 2), 2 * D_C // MXU_COLS, m, MXU_COLS), F32),
            pltpu.VMEM((groups, P_SLAB, SEGS * POOL_PITCH, LANES), F32),
            pltpu.VMEM((groups, P_SLAB, SEGS * OUT_PITCH, LANES), F32),
            pltpu.VMEM((nb, POOL_HIST, D_C), F32),
            pltpu.VMEM((m, D_C), BF16),
            pltpu.VMEM((m, D_C), BF16),
        ],
        compiler_params=pltpu.CompilerParams(
            dimension_semantics=("arbitrary", "arbitrary"), vmem_limit_bytes=VMEM_LIMIT),
        name="odd_layer",
    )(x, mods, pool_st, ng, w_in, gw, gb, sc, w_out, fg)


PROMPT_TILE = dict(nb=4, tl=256, tps=1)
SAMPLE_TILE = dict(nb=16, tl=32, tps=1)


def kernel(x_prompt, x_sample, c_prompt, c_sample, state_conv, state_gla, state_pool, norm_g, ada_w, ada_b, ev_w_in, ev_conv_w, ev_conv_b, ev_ln_g, ev_ln_b, ev_gate_w2, ev_gate_b, ev_head_g, ev_w_out, od_w_in, od_group_w, od_group_b, od_scale, od_w_out, final_g):
    b_p = x_prompt.shape[0]
    b_s = x_sample.shape[0]
    n_even = ev_w_in.shape[0]
    n_odd = od_w_in.shape[0]

    c_all = jnp.concatenate([c_sample, c_prompt], axis=0)
    bp = -(-c_all.shape[0] // SUBLANES) * SUBLANES
    c_all = jnp.pad(c_all, ((0, bp - c_all.shape[0]), (0, 0)))
    mods = _ada_call(c_all, ada_w, ada_b).reshape(DEPTH, 3, bp, 1, D_MODEL)

    ev_w_in_b = jnp.pad(ev_w_in.astype(BF16), ((0, 0), (0, 0), (0, EV_IN_PAD - EV_IN)))
    ev_gw2_b = jnp.pad(ev_gate_w2.astype(BF16), ((0, 0), (0, LANES - GATE_RANK), (0, 0)))
    ev_w_out_b = ev_w_out.astype(BF16)
    od_w_in_b = od_w_in.astype(BF16)
    od_gw_b = od_group_w.astype(BF16)
    od_w_out_b = od_w_out.astype(BF16)

    zero_conv = jnp.zeros((1, b_p, CONV_PAD, D_A), F32)
    zero_gla = jnp.zeros((1, b_p, H_B, DK_B, DV_B), F32)
    zero_pool = jnp.zeros((1, b_p, POOL_PAD, D_C), F32)

    def rows(a):
        return a.reshape(a.shape[0], 1, a.shape[1])

    ng = rows(norm_g)
    ev_w = (ng, ev_w_in_b, ev_conv_w, rows(ev_conv_b), rows(ev_ln_g), rows(ev_ln_b), ev_gw2_b,
            rows(ev_gate_b), rows(ev_head_g), ev_w_out_b)
    od_w = (ng, od_w_in_b, od_gw_b, rows(od_group_b), rows(od_scale), od_w_out_b,
            final_g.reshape(1, D_MODEL))
    prompt = dict(mod_row0=b_s, state_layer=0, **PROMPT_TILE)

    xp, xs = x_prompt, x_sample
    conv_p, gla_p, pool_p, conv_s, gla_s, pool_s = [], [], [], [], [], []
    for l in range(DEPTH):
        sample = dict(mod_row0=0, state_layer=l // 2, **SAMPLE_TILE)
        if l % 2 == 0:
            xp, cp, gp = _even_call(xp, mods, zero_conv, zero_gla, *ev_w, layer=l, **prompt)
            xs, cs, gs = _even_call(xs, mods, state_conv, state_gla, *ev_w, layer=l, **sample)
            conv_p.append(cp)
            gla_p.append(gp)
            conv_s.append(cs)
            gla_s.append(gs)
        else:
            xp, pp = _odd_call(xp, mods, zero_pool, *od_w, layer=l, pos0=0, **prompt)
            xs, ps = _odd_call(xs, mods, state_pool, *od_w, layer=l, pos0=PAST_LEN, **sample)
            pool_p.append(pp)
            pool_s.append(ps)
    assert n_even == len(conv_p) and n_odd == len(pool_p)
    return (xp, xs, jnp.stack(conv_p), jnp.stack(gla_p), jnp.stack(pool_p),
            jnp.stack(conv_s), jnp.stack(gla_s), jnp.stack(pool_s))
```
